```python
import math
import jax, jax.numpy as jnp
from jax import lax
import numpy as np

D_MODEL = 1024
BATCH = 2
SEQ = 8192
DEPTH = 2
DEC_BATCH = 32
DEC_SEQ = 4
PAST_LEN = 8192
PAGE_SIZE = 128

D_MIX = D_MODEL
ATT_HEADS = 8
ATT_HEAD_DIM = 64
ATT_WIDTH = ATT_HEADS * ATT_HEAD_DIM
DN_WIDTH = D_MIX - ATT_WIDTH
DN_HEADS = 4
DN_HEAD_DIM = DN_WIDTH // DN_HEADS
DN_QKV = 3 * DN_WIDTH
D_IN = 4 * ATT_WIDTH + DN_QKV + DN_WIDTH + 2 * DN_HEADS
MOBA_BLOCK = 256
MOBA_TOPK = 3
Q_BLOCK = 64
DN_CHUNK = 64
CONV_WIDTH = 4
ROPE_THETA = 10000.0
DEEPNORM_ALPHA = (2 * DEPTH) ** 0.25
DEEPNORM_BETA = (8 * DEPTH) ** -0.25
LN_EPS = 1e-5
RMS_EPS = 1e-6
L2_EPS = 1e-6
ADA_SCALE = 0.5

kernel_name = 'hymba_moba_gdn_deepnorm_adaln_step'


def _rope(x, pos):
    half = x.shape[-1] // 2
    inv_freq = ROPE_THETA ** (-jnp.arange(half, dtype=jnp.float32) / half)
    ang = pos.astype(jnp.float32)[:, None] * inv_freq[None, :]
    cos = jnp.cos(ang)[:, None, :]
    sin = jnp.sin(ang)[:, None, :]
    xf = x.astype(jnp.float32)
    x1, x2 = xf[..., :half], xf[..., half:]
    return jnp.concatenate([x1 * cos - x2 * sin, x1 * sin + x2 * cos], axis=-1).astype(x.dtype)


def _l2norm(a):
    return a * lax.rsqrt(jnp.sum(a * a, axis=-1, keepdims=True) + L2_EPS)


def _moba_attention(q, k, v, q_pos):
    N, T, H, Dh = q.shape
    L = k.shape[1]
    nb = -(-L // MOBA_BLOCK)
    pad = nb * MOBA_BLOCK - L
    kb = jnp.pad(k, ((0, 0), (0, pad), (0, 0), (0, 0))).reshape(N, nb, MOBA_BLOCK, H, Dh)
    vb = jnp.pad(v, ((0, 0), (0, pad), (0, 0), (0, 0))).reshape(N, nb, MOBA_BLOCK, H, Dh)
    k_mean = jnp.mean(kb.astype(jnp.float32), axis=2)
    kb = kb.transpose(0, 3, 1, 2, 4)
    vb = vb.transpose(0, 3, 1, 2, 4)
    n_sel = min(MOBA_TOPK, nb)
    scale = Dh ** -0.5
    n_ix = jnp.arange(N)[:, None, None, None]
    h_ix = jnp.arange(H)[None, None, :, None]
    blk = jnp.arange(nb, dtype=jnp.int32)
    offs = jnp.arange(MOBA_BLOCK, dtype=jnp.int32)
    cq = Q_BLOCK if T % Q_BLOCK == 0 else T
    nq = T // cq

    def one_block(args):
        qc, pc = args
        qf = qc.astype(jnp.float32)
        own = pc // MOBA_BLOCK
        gate = jnp.einsum('nchd,njhd->nchj', qf, k_mean)
        past_ok = blk[None, :] < own[:, None]
        gate = jnp.where(past_ok[None, :, None, :], gate, -jnp.inf)
        _, top = lax.top_k(gate, n_sel)
        own_b = jnp.broadcast_to(own[None, :, None, None], (N, cq, H, 1)).astype(top.dtype)
        sel = jnp.concatenate([top, own_b], axis=-1)
        kg = kb[n_ix, h_ix, sel].astype(jnp.float32)
        vg = vb[n_ix, h_ix, sel].astype(jnp.float32)
        logits = jnp.einsum('nchd,nchsbd->nchsb', qf, kg) * scale
        slot_ok = jnp.concatenate(
            [jnp.arange(n_sel)[None, :] < own[:, None], jnp.ones((cq, 1), dtype=bool)], axis=-1)
        kpos = sel[..., None] * MOBA_BLOCK + offs
        ok = slot_ok[None, :, None, :, None] & (kpos <= pc[None, :, None, None, None])
        logits = jnp.where(ok, logits, -jnp.inf)
        p = jax.nn.softmax(logits.reshape(N, cq, H, -1), axis=-1).reshape(logits.shape)
        out = jnp.einsum('nchsb,nchsbd->nchd', p, vg)
        return out.astype(q.dtype)

    qs = q.reshape(N, nq, cq, H, Dh).transpose(1, 0, 2, 3, 4)
    ps = q_pos.reshape(nq, cq)
    out = lax.map(one_block, (qs, ps))
    return out.transpose(1, 0, 2, 3, 4).reshape(N, T, H, Dh)


def _gated_delta_rule(q, k, v, g, beta, S0):
    N, T, H, Dk = q.shape
    Dv = v.shape[-1]
    C = min(DN_CHUNK, T)
    nc = -(-T // C)
    pad = nc * C - T

    def prep(a):
        a = jnp.pad(a, [(0, 0), (0, pad)] + [(0, 0)] * (a.ndim - 2))
        a = a.reshape((N, nc, C) + a.shape[2:])
        a = jnp.moveaxis(a, 3, 2)
        return jnp.moveaxis(a, 1, 0)

    qc, kc, vc, gc, bc = prep(q), prep(k), prep(v), prep(g), prep(beta)
    gcum = jnp.cumsum(gc, axis=-1)
    tril = jnp.tril(jnp.ones((C, C), dtype=bool))
    stril = jnp.tril(jnp.ones((C, C), dtype=bool), -1)
    diff = gcum[..., :, None] - gcum[..., None, :]
    decay = jnp.where(tril, jnp.exp(jnp.where(tril, diff, 0.0)), 0.0)
    kbeta = kc * bc[..., None]
    lmat = jnp.where(stril, jnp.einsum('...id,...jd->...ij', kbeta, kc) * decay, 0.0)
    eye = jnp.eye(C, dtype=jnp.float32)
    tmat = lax.linalg.triangular_solve(eye + lmat, jnp.broadcast_to(eye, lmat.shape),
                                       left_side=True, lower=True, unit_diagonal=True)
    u = tmat @ (vc * bc[..., None])
    w = tmat @ (kbeta * jnp.exp(gcum)[..., None])
    attn = jnp.where(tril, jnp.einsum('...id,...jd->...ij', qc, kc) * decay, 0.0)

    def step(S, xs):
        q_i, k_i, u_i, w_i, g_i, a_i = xs
        v_new = u_i - w_i @ S
        o = (q_i * jnp.exp(g_i)[..., None]) @ S + a_i @ v_new
        g_last = g_i[..., -1]
        S = S * jnp.exp(g_last)[..., None, None] + jnp.einsum(
            'nhck,nhcv->nhkv', k_i * jnp.exp(g_last[..., None] - g_i)[..., None], v_new)
        return S, o

    S, o = lax.scan(step, S0, (qc, kc, u, w, gcum, attn))
    o = jnp.moveaxis(o, 0, 1)
    o = jnp.swapaxes(o, 2, 3).reshape(N, nc * C, H, Dv)[:, :T]
    return o, S


def _layer(x, c, pos0, k_past, v_past, S0, conv0,
           w_ada, b_ada, w_in, conv_w, a_log, dt_bias, dn_norm_w, w_out, ln_g, ln_b):
    N, T, _ = x.shape
    f32 = jnp.float32
    mod = jax.nn.silu(c) @ w_ada + b_ada
    shift, scale, gate = jnp.split(mod, 3, axis=-1)
    h = x * (1 + scale[:, None, :]) + shift[:, None, :]
    proj = h @ w_in
    sizes = [ATT_WIDTH] * 4 + [DN_QKV, DN_WIDTH, DN_HEADS, DN_HEADS]
    cuts = [int(s) for s in np.cumsum(sizes)[:-1]]
    qa, ka, va, za, qkv_d, zd, b_raw, a_raw = jnp.split(proj, cuts, axis=-1)
    positions = pos0 + jnp.arange(T, dtype=jnp.int32)

    qa = _rope(qa.reshape(N, T, ATT_HEADS, ATT_HEAD_DIM), positions)
    ka = _rope(ka.reshape(N, T, ATT_HEADS, ATT_HEAD_DIM), positions)
    va = va.reshape(N, T, ATT_HEADS, ATT_HEAD_DIM)
    k_all = jnp.concatenate([k_past.astype(ka.dtype), ka], axis=1)
    v_all = jnp.concatenate([v_past.astype(va.dtype), va], axis=1)
    oa = _moba_attention(qa, k_all, v_all, positions).reshape(N, T, ATT_WIDTH)
    ya = oa * jax.nn.silu(za)

    xpad = jnp.concatenate([conv0.astype(qkv_d.dtype), qkv_d], axis=1)
    conv = sum(xpad[:, i:i + T] * conv_w[i] for i in range(CONV_WIDTH))
    conv = jax.nn.silu(conv.astype(f32))
    new_conv = xpad[:, T:]
    qd, kd, vd = jnp.split(conv, 3, axis=-1)
    qd = _l2norm(qd.reshape(N, T, DN_HEADS, DN_HEAD_DIM)) * (DN_HEAD_DIM ** -0.5)
    kd = _l2norm(kd.reshape(N, T, DN_HEADS, DN_HEAD_DIM))
    vd = vd.reshape(N, T, DN_HEADS, DN_HEAD_DIM)
    beta = jax.nn.sigmoid(b_raw.astype(f32))
    g = -jnp.exp(a_log.astype(f32)) * jax.nn.softplus(a_raw.astype(f32) + dt_bias.astype(f32))
    od, S_new = _gated_delta_rule(qd, kd, vd, g, beta, S0.astype(f32))
    od = od * lax.rsqrt(jnp.mean(od * od, axis=-1, keepdims=True) + RMS_EPS) * dn_norm_w.astype(f32)
    yd = (od.reshape(N, T, DN_WIDTH) * jax.nn.silu(zd.astype(f32))).astype(x.dtype)

    y = jnp.concatenate([ya.astype(x.dtype), yd], axis=-1) @ w_out
    r = (DEEPNORM_ALPHA * x + (1 + gate[:, None, :]) * y).astype(f32)
    mu = jnp.mean(r, axis=-1, keepdims=True)
    var = jnp.mean(jnp.square(r - mu), axis=-1, keepdims=True)
    out = (r - mu) * lax.rsqrt(var + LN_EPS) * ln_g.astype(f32) + ln_b.astype(f32)
    return out.astype(x.dtype), ka, va, S_new, new_conv


def setup_inputs(seed: int = 0) -> dict:
    key = jax.random.key(seed)
    ks = jax.random.split(key, 24)
    f32 = jnp.float32
    n_pages = PAST_LEN // PAGE_SIZE
    n_pool = (DEC_BATCH * n_pages * 5) // 4
    nrm = jax.random.normal
    x_prompt = nrm(ks[0], (BATCH, SEQ, D_MODEL), f32)
    x_sample = nrm(ks[1], (DEC_BATCH, DEC_SEQ, D_MODEL), f32)
    c_prompt = nrm(ks[2], (BATCH, D_MODEL), f32)
    c_sample = nrm(ks[3], (DEC_BATCH, D_MODEL), f32)
    cache_k = nrm(ks[4], (DEPTH, n_pool, PAGE_SIZE, ATT_HEADS, ATT_HEAD_DIM), f32)
    cache_v = nrm(ks[5], (DEPTH, n_pool, PAGE_SIZE, ATT_HEADS, ATT_HEAD_DIM), f32)
    page_table = jax.random.permutation(ks[6], n_pool)[:DEC_BATCH * n_pages].reshape(
        DEC_BATCH, n_pages).astype(jnp.int32)
    state_ssm = 0.1 * nrm(ks[7], (DEPTH, DEC_BATCH, DN_HEADS, DN_HEAD_DIM, DN_HEAD_DIM), f32)
    state_conv = nrm(ks[8], (DEPTH, DEC_BATCH, CONV_WIDTH - 1, DN_QKV), f32)
    w_ada = nrm(ks[9], (DEPTH, D_MODEL, 3 * D_MODEL), f32) * (ADA_SCALE * D_MODEL ** -0.5)
    b_ada = 0.02 * nrm(ks[10], (DEPTH, 3 * D_MODEL), f32)
    w_in = nrm(ks[11], (DEPTH, D_MODEL, D_IN), f32) * (D_MODEL ** -0.5)
    conv_w = nrm(ks[12], (DEPTH, CONV_WIDTH, DN_QKV), f32) * (CONV_WIDTH ** -0.5)
    a_log = jnp.log(jax.random.uniform(ks[13], (DEPTH, DN_HEADS), f32, 1.0, 16.0))
    dt = jnp.exp(jax.random.uniform(ks[14], (DEPTH, DN_HEADS), f32, math.log(1e-3), math.log(1e-1)))
    dt_bias = dt + jnp.log(-jnp.expm1(-dt))
    dn_norm_w = 1.0 + 0.02 * nrm(ks[15], (DEPTH, DN_HEAD_DIM), f32)
    w_out = nrm(ks[16], (DEPTH, D_MIX, D_MODEL), f32) * (D_MIX ** -0.5 * DEEPNORM_BETA)
    ln_g = 1.0 + 0.02 * nrm(ks[17], (DEPTH, D_MODEL), f32)
    ln_b = 0.02 * nrm(ks[18], (DEPTH, D_MODEL), f32)
    return {'x_prompt': x_prompt, 'x_sample': x_sample, 'c_prompt': c_prompt, 'c_sample': c_sample,
            'cache_k': cache_k, 'cache_v': cache_v, 'page_table': page_table,
            'state_ssm': state_ssm, 'state_conv': state_conv,
            'w_ada': w_ada, 'b_ada': b_ada, 'w_in': w_in, 'conv_w': conv_w, 'a_log': a_log,
            'dt_bias': dt_bias, 'dn_norm_w': dn_norm_w, 'w_out': w_out, 'ln_g': ln_g, 'ln_b': ln_b}


def reference(x_prompt, x_sample, c_prompt, c_sample, cache_k, cache_v, page_table,
              state_ssm, state_conv, w_ada, b_ada, w_in, conv_w, a_log, dt_bias,
              dn_norm_w, w_out, ln_g, ln_b):
    n_pages = PAST_LEN // PAGE_SIZE
    bp = x_prompt.shape[0]
    bs = x_sample.shape[0]
    xp, xs = x_prompt, x_sample
    kp_l, vp_l, sp_l, cp_l = [], [], [], []
    ks_l, vs_l, ss_l, cs_l = [], [], [], []
    for l in range(DEPTH):
        lw = (w_ada[l], b_ada[l], w_in[l], conv_w[l], a_log[l], dt_bias[l],
              dn_norm_w[l], w_out[l], ln_g[l], ln_b[l])
        k0 = jnp.zeros((bp, 0, ATT_HEADS, ATT_HEAD_DIM), xp.dtype)
        s0 = jnp.zeros((bp, DN_HEADS, DN_HEAD_DIM, DN_HEAD_DIM), jnp.float32)
        c0 = jnp.zeros((bp, CONV_WIDTH - 1, DN_QKV), xp.dtype)
        xp, kp, vp, sp, cp = _layer(xp, c_prompt, 0, k0, k0, s0, c0, *lw)
        k_past = cache_k[l][page_table].reshape(bs, n_pages * PAGE_SIZE, ATT_HEADS, ATT_HEAD_DIM)
        v_past = cache_v[l][page_table].reshape(bs, n_pages * PAGE_SIZE, ATT_HEADS, ATT_HEAD_DIM)
        xs, kn, vn, sn, cn = _layer(xs, c_sample, n_pages * PAGE_SIZE, k_past, v_past,
                                    state_ssm[l], state_conv[l], *lw)
        kp_l.append(kp); vp_l.append(vp); sp_l.append(sp.astype(state_ssm.dtype)); cp_l.append(cp)
        ks_l.append(kn); vs_l.append(vn); ss_l.append(sn.astype(state_ssm.dtype)); cs_l.append(cn)
    new_k_prompt = jnp.stack(kp_l)
    new_v_prompt = jnp.stack(vp_l)
    new_ssm_prompt = jnp.stack(sp_l)
    new_conv_prompt = jnp.stack(cp_l)
    new_k_sample = jnp.stack(ks_l)
    new_v_sample = jnp.stack(vs_l)
    new_ssm_sample = jnp.stack(ss_l)
    new_conv_sample = jnp.stack(cs_l)
    return (xp, xs, new_k_prompt, new_v_prompt, new_ssm_prompt, new_conv_prompt,
            new_k_sample, new_v_sample, new_ssm_sample, new_conv_sample)
```

```python
import functools
import math

import jax
import jax.numpy as jnp
from jax import lax
from jax.experimental import pallas as pl
from jax.experimental.pallas import tpu as pltpu

F32 = jnp.float32
BF16 = jnp.bfloat16
HIGHEST = lax.Precision.HIGHEST

ATT_HEADS = 8
ATT_HEAD_DIM = 64
ATT_WIDTH = ATT_HEADS * ATT_HEAD_DIM
DN_HEADS = 4
DN_HEAD_DIM = 128
DN_WIDTH = DN_HEADS * DN_HEAD_DIM
DN_QKV = 3 * DN_WIDTH
MAIN_COLS = 4 * ATT_WIDTH + DN_QKV + DN_WIDTH
MOBA_BLOCK = 256
MOBA_TOPK = 3
DN_CHUNK = 64
CONV_WIDTH = 4
PAGE_SIZE = 128
ROPE_THETA = 10000.0
LN_EPS = 1e-5
RMS_EPS = 1e-6
L2_EPS = 1e-6
ATT_SCALE = ATT_HEAD_DIM ** -0.5
NEG = -1e30
LANES = 128
SUBLANES = 8
PAGES_PER_STEP = 8
VMEM_LIMIT = 56 * 1024 * 1024

NT_DIMS = (((1,), (1,)), ((), ()))
TN_DIMS = (((0,), (0,)), ((), ()))


def _sigmoid(x):
    return 1.0 / (1.0 + jnp.exp(-x))


def _silu(x):
    return x * _sigmoid(x)


def _softplus(x):
    return jnp.maximum(x, 0.0) + jnp.log1p(jnp.exp(-jnp.abs(x)))


def _cparams(sem):
    return pltpu.CompilerParams(dimension_semantics=sem, vmem_limit_bytes=VMEM_LIMIT)


def _mod_kernel(c_ref, w_ref, b_ref, o_ref):
    s = _silu(c_ref[...])
    o_ref[0] = jnp.dot(s.astype(BF16), w_ref[0].astype(BF16), preferred_element_type=F32) + b_ref[0]


def _mod_call(c_all, w_ada, b_ada):
    depth, d, d3 = w_ada.shape
    rows = c_all.shape[0]
    nj = d3 // d
    return pl.pallas_call(
        _mod_kernel,
        grid=(depth, nj),
        in_specs=[
            pl.BlockSpec((rows, d), lambda l, j: (0, 0)),
            pl.BlockSpec((1, d, d), lambda l, j: (l, 0, j)),
            pl.BlockSpec((1, 1, d), lambda l, j: (l, 0, j)),
        ],
        out_specs=pl.BlockSpec((1, rows, d), lambda l, j: (l, 0, j)),
        out_shape=jax.ShapeDtypeStruct((depth, rows, d3), F32),
        compiler_params=_cparams(("arbitrary", "arbitrary")),
        name="mod",
    )(c_all, w_ada, b_ada.reshape(depth, 1, d3))


def _rope(p, cos, sin_signed):
    lane = lax.broadcasted_iota(jnp.int32, (1, LANES), 1)
    first_half = (lane % ATT_HEAD_DIM) < (ATT_HEAD_DIM // 2)
    outs = []
    for j in range(ATT_WIDTH // LANES):
        xj = p[:, j * LANES:(j + 1) * LANES]
        partner = jnp.where(first_half, pltpu.roll(xj, LANES - 32, 1), pltpu.roll(xj, 32, 1))
        outs.append(xj * cos + partner * sin_signed)
    return jnp.concatenate(outs, axis=-1)


def _l2norm_heads(c, scale):
    outs = []
    for h in range(DN_HEADS):
        ch = c[:, h * DN_HEAD_DIM:(h + 1) * DN_HEAD_DIM]
        ss = jnp.sum(ch * ch, axis=-1, keepdims=True)
        outs.append(ch * (lax.rsqrt(ss + L2_EPS) * scale))
    return jnp.concatenate(outs, axis=-1)


def _beta_decay(h, ws_ref, alog_ref, dtb_ref):
    raw = jnp.dot(h, ws_ref[...], precision=HIGHEST, preferred_element_type=F32)
    lane = lax.broadcasted_iota(jnp.int32, (1, LANES), 1)
    beta = _sigmoid(raw)
    g = -jnp.exp(alog_ref[...]) * _softplus(raw + dtb_ref[...])
    return jnp.where(lane < DN_HEADS, beta, jnp.where(lane < 2 * DN_HEADS, g, 0.0))


def _proj(hb, w_ref, g):
    return jnp.dot(hb, w_ref[:, g * ATT_WIDTH:(g + 1) * ATT_WIDTH], preferred_element_type=F32)


def _in_prompt_kernel(x_ref, mod_ref, w_ref, ws_ref, cw_ref, alog_ref, dtb_ref, cos_ref, sin_ref,
                      q_ref, k_ref, v_ref, kb_ref, vb_ref, km_ref, sza_ref,
                      qd_ref, kd_ref, vd_ref, szd_ref, gb_ref, nc_ref, xbuf, *, tm):
    t = pl.program_id(1)
    d = x_ref.shape[-1]
    hist = SUBLANES

    @pl.when(t == 0)
    def _():
        xbuf[0:hist, :] = jnp.zeros((hist, DN_QKV), F32)

    x = x_ref[0]
    shift = mod_ref[0][:, 0:d]
    scale = mod_ref[0][:, d:2 * d]
    h = x * (1.0 + scale) + shift
    hb = h.astype(BF16)
    cos = cos_ref[...]
    sin = sin_ref[...]

    q_ref[0] = _rope(_proj(hb, w_ref, 0), cos, sin)
    kr = _rope(_proj(hb, w_ref, 1), cos, sin)
    k_ref[0] = kr
    kb_ref[0] = kr.astype(BF16)
    km_ref[0, 0] = jnp.sum(kr, axis=0, keepdims=True) * (1.0 / tm)
    v = _proj(hb, w_ref, 2)
    v_ref[0] = v
    vb_ref[0] = v.astype(BF16)
    sza_ref[0] = _silu(_proj(hb, w_ref, 3))
    szd_ref[0] = _silu(_proj(hb, w_ref, 7))
    gb_ref[0] = _beta_decay(h, ws_ref, alog_ref, dtb_ref)

    outs = (qd_ref, kd_ref, vd_ref)
    for gi in range(3):
        cols = slice(gi * DN_WIDTH, (gi + 1) * DN_WIDTH)
        xbuf[hist:hist + tm, cols] = _proj(hb, w_ref, 4 + gi)
        conv = jnp.zeros((tm, DN_WIDTH), F32)
        for i in range(CONV_WIDTH):
            off = hist - (CONV_WIDTH - 1) + i
            conv = conv + xbuf[off:off + tm, cols] * cw_ref[i:i + 1, cols]
        c = _silu(conv)
        if gi == 0:
            c = _l2norm_heads(c, DN_HEAD_DIM ** -0.5)
        elif gi == 1:
            c = _l2norm_heads(c, 1.0)
        outs[gi][0] = c

    tail = xbuf[tm:tm + hist, :]
    nc_ref[0] = tail
    xbuf[0:hist, :] = tail


def _in_prompt_call(x, mod, w_main, w_small, conv_w, alog, dtb, cos, sin, tm):
    n, t, d = x.shape
    nt = t // tm
    row = lambda n_, t_: (n_, t_, 0)
    full = lambda n_, t_: (0, 0)
    wide = jax.ShapeDtypeStruct((n, t, ATT_WIDTH), F32)
    wide_b = jax.ShapeDtypeStruct((n, t, ATT_WIDTH), BF16)
    blk = pl.BlockSpec((1, tm, ATT_WIDTH), row)
    return pl.pallas_call(
        functools.partial(_in_prompt_kernel, tm=tm),
        grid=(n, nt),
        in_specs=[
            pl.BlockSpec((1, tm, d), row),
            pl.BlockSpec((1, 1, 3 * d), lambda n_, t_: (n_, 0, 0)),
            pl.BlockSpec((d, MAIN_COLS), full),
            pl.BlockSpec((d, LANES), full),
            pl.BlockSpec((CONV_WIDTH, DN_QKV), full),
            pl.BlockSpec((1, LANES), full),
            pl.BlockSpec((1, LANES), full),
            pl.BlockSpec((tm, LANES), lambda n_, t_: (t_, 0)),
            pl.BlockSpec((tm, LANES), lambda n_, t_: (t_, 0)),
        ],
        out_specs=[
            blk, blk, blk, blk, blk,
            pl.BlockSpec((1, 1, 1, ATT_WIDTH), lambda n_, t_: (n_, t_, 0, 0)),
            blk, blk, blk, blk, blk,
            pl.BlockSpec((1, tm, LANES), row),
            pl.BlockSpec((1, SUBLANES, DN_QKV), lambda n_, t_: (n_, 0, 0)),
        ],
        out_shape=[
            wide, wide, wide, wide_b, wide_b,
            jax.ShapeDtypeStruct((n, nt, 1, ATT_WIDTH), F32),
            wide, wide, wide, wide, wide,
            jax.ShapeDtypeStruct((n, t, LANES), F32),
            jax.ShapeDtypeStruct((n, SUBLANES, DN_QKV), F32),
        ],
        scratch_shapes=[pltpu.VMEM((tm + SUBLANES, DN_QKV), F32)],
        compiler_params=_cparams(("arbitrary", "arbitrary")),
        name="in_prompt",
    )(x, mod, w_main, w_small, conv_w, alog, dtb, cos, sin)


def _in_sample_kernel(x_ref, shift_ref, scale_ref, w_ref, ws_ref, cw_ref, alog_ref, dtb_ref,
                      cos_ref, sin_ref, hist_ref,
                      q_ref, k_ref, v_ref, sza_ref, qd_ref, kd_ref, vd_ref, szd_ref, gb_ref, nc_ref,
                      xall, *, batch, steps):
    rows = batch * steps
    nh = (CONV_WIDTH - 1) * batch
    h = x_ref[...] * (1.0 + scale_ref[...]) + shift_ref[...]
    hb = h.astype(BF16)
    cos = cos_ref[...]
    sin = sin_ref[...]
    q_ref[...] = _rope(_proj(hb, w_ref, 0), cos, sin)
    k_ref[...] = _rope(_proj(hb, w_ref, 1), cos, sin)
    v_ref[...] = _proj(hb, w_ref, 2)
    sza_ref[...] = _silu(_proj(hb, w_ref, 3))
    szd_ref[...] = _silu(_proj(hb, w_ref, 7))
    gb_ref[...] = _beta_decay(h, ws_ref, alog_ref, dtb_ref)

    xall[0:nh, :] = hist_ref[...]
    outs = (qd_ref, kd_ref, vd_ref)
    for gi in range(3):
        cols = slice(gi * DN_WIDTH, (gi + 1) * DN_WIDTH)
        xall[nh:nh + rows, cols] = _proj(hb, w_ref, 4 + gi)
        conv = jnp.zeros((rows, DN_WIDTH), F32)
        for i in range(CONV_WIDTH):
            conv = conv + xall[i * batch:i * batch + rows, cols] * cw_ref[i:i + 1, cols]
        c = _silu(conv)
        if gi == 0:
            c = _l2norm_heads(c, DN_HEAD_DIM ** -0.5)
        elif gi == 1:
            c = _l2norm_heads(c, 1.0)
        outs[gi][...] = c
    nc_ref[...] = xall[rows:rows + nh, :]


def _in_sample_call(x_tm, shift, scale, w_main, w_small, conv_w, alog, dtb, cos, sin, hist_tm, batch, steps):
    rows = batch * steps
    nh = (CONV_WIDTH - 1) * batch
    wide = jax.ShapeDtypeStruct((rows, ATT_WIDTH), F32)
    return pl.pallas_call(
        functools.partial(_in_sample_kernel, batch=batch, steps=steps),
        out_shape=[wide] * 8 + [jax.ShapeDtypeStruct((rows, LANES), F32),
                                jax.ShapeDtypeStruct((nh, DN_QKV), F32)],
        scratch_shapes=[pltpu.VMEM((rows + nh, DN_QKV), F32)],
        compiler_params=pltpu.CompilerParams(vmem_limit_bytes=VMEM_LIMIT),
        name="in_sample",
    )(x_tm, shift, scale, w_main, w_small, conv_w, alog, dtb, cos, sin, hist_tm)


def _topk_bias(gate, valid, blkf):
    gate = jnp.where(valid, gate, -jnp.inf)
    bias = jnp.full(gate.shape, NEG, F32)
    for _ in range(MOBA_TOPK):
        mx = jnp.max(gate, axis=-1, keepdims=True)
        cand = (gate == mx) & (mx > -jnp.inf)
        idx = jnp.min(jnp.where(cand, blkf, 1e9), axis=-1, keepdims=True)
        pick = blkf == idx
        bias = jnp.where(pick, 0.0, bias)
        gate = jnp.where(pick, -jnp.inf, gate)
    return bias


def _moba_prompt_kernel(q_ref, k_ref, v_ref, km_ref, o_ref, m_ref, l_ref, acc_ref):
    i = pl.program_id(2)
    bs = MOBA_BLOCK
    nb = km_ref.shape[1]
    q = q_ref[0]
    km = km_ref[0]
    lane = lax.broadcasted_iota(jnp.int32, (1, LANES), 1)
    blk = lax.broadcasted_iota(jnp.int32, (bs, nb), 1)
    blkf = blk.astype(F32)
    row = lax.broadcasted_iota(jnp.int32, (bs, bs), 0)
    col = lax.broadcasted_iota(jnp.int32, (bs, bs), 1)
    start = pl.multiple_of(i * bs, bs)
    k_own = k_ref[0, pl.ds(start, bs), :]
    v_own = v_ref[0, pl.ds(start, bs), :]

    biases, qbs = [], []
    for hh in range(2):
        qh = jnp.where((lane // ATT_HEAD_DIM) == hh, q, 0.0)
        gate = lax.dot_general(qh, km, NT_DIMS, precision=HIGHEST, preferred_element_type=F32)
        biases.append(_topk_bias(gate, blk < i, blkf))
        qb = (qh * ATT_SCALE).astype(BF16)
        qbs.append(qb)
        s = lax.dot_general(qb, k_own, NT_DIMS, preferred_element_type=F32)
        s = jnp.where(col <= row, s, NEG)
        m = jnp.max(s, axis=-1, keepdims=True)
        p = jnp.exp(s - m)
        m_ref[hh] = m
        l_ref[hh] = jnp.sum(p, axis=-1, keepdims=True)
        acc_ref[hh] = jnp.dot(p.astype(BF16), v_own, preferred_element_type=F32)

    def body(j, carry):
        off = pl.multiple_of(j * bs, bs)
        kj = k_ref[0, pl.ds(off, bs), :]
        vj = v_ref[0, pl.ds(off, bs), :]
        for hh in range(2):
            bcol = jnp.sum(jnp.where(blk == j, biases[hh], 0.0), axis=-1, keepdims=True)
            s = lax.dot_general(qbs[hh], kj, NT_DIMS, preferred_element_type=F32) + bcol
            m_old = m_ref[hh]
            m_new = jnp.maximum(m_old, jnp.max(s, axis=-1, keepdims=True))
            alpha = jnp.exp(m_old - m_new)
            p = jnp.exp(s - m_new)
            l_ref[hh] = alpha * l_ref[hh] + jnp.sum(p, axis=-1, keepdims=True)
            acc_ref[hh] = alpha * acc_ref[hh] + jnp.dot(p.astype(BF16), vj, preferred_element_type=F32)
            m_ref[hh] = m_new
        return carry

    lax.fori_loop(0, i, body, 0)
    o_ref[0] = jnp.where((lane // ATT_HEAD_DIM) == 0, acc_ref[0] / l_ref[0], acc_ref[1] / l_ref[1])


def _moba_prompt_call(q, k_bf, v_bf, kmean):
    n, t, _ = q.shape
    nb = t // MOBA_BLOCK
    npair = ATT_WIDTH // LANES
    return pl.pallas_call(
        _moba_prompt_kernel,
        grid=(n, npair, nb),
        in_specs=[
            pl.BlockSpec((1, MOBA_BLOCK, LANES), lambda n_, p_, i_: (n_, i_, p_)),
            pl.BlockSpec((1, t, LANES), lambda n_, p_, i_: (n_, 0, p_)),
            pl.BlockSpec((1, t, LANES), lambda n_, p_, i_: (n_, 0, p_)),
            pl.BlockSpec((1, nb, LANES), lambda n_, p_, i_: (n_, 0, p_)),
        ],
        out_specs=pl.BlockSpec((1, MOBA_BLOCK, LANES), lambda n_, p_, i_: (n_, i_, p_)),
        out_shape=jax.ShapeDtypeStruct((n, t, ATT_WIDTH), F32),
        scratch_shapes=[
            pltpu.VMEM((2, MOBA_BLOCK, 1), F32),
            pltpu.VMEM((2, MOBA_BLOCK, 1), F32),
            pltpu.VMEM((2, MOBA_BLOCK, LANES), F32),
        ],
        compiler_params=_cparams(("arbitrary", "arbitrary", "arbitrary")),
        name="moba_prompt",
    )(q, k_bf, v_bf, kmean)


def _moba_sample_kernel(pt_ref, q_ref, kn_ref, vn_ref, *rest, npages):
    pps = PAGES_PER_STEP
    kp = rest[:pps]
    vp = rest[pps:2 * pps]
    o_ref = rest[2 * pps]
    lg_ref, p_ref, km_ref, acc_ref, lsum_ref, qe_ref = rest[2 * pps + 1:]
    s = pl.program_id(1)
    nsteps = npages // pps
    nblk = npages * PAGE_SIZE // MOBA_BLOCK
    ppb = MOBA_BLOCK // PAGE_SIZE
    rows = qe_ref.shape[0]
    steps = rows // ATT_HEADS
    lane = lax.broadcasted_iota(jnp.int32, (ATT_HEADS, ATT_WIDTH), 1)
    hrow = lax.broadcasted_iota(jnp.int32, (ATT_HEADS, ATT_WIDTH), 0)
    head_mask = (lane // ATT_HEAD_DIM) == hrow

    @pl.when(s == 0)
    def _():
        q = q_ref[0]
        for t in range(steps):
            qe_ref[t * ATT_HEADS:(t + 1) * ATT_HEADS, :] = jnp.where(head_mask, q[t:t + 1, :], 0.0)

    @pl.when(s < nsteps)
    def _():
        qe = (qe_ref[...] * ATT_SCALE).astype(BF16)
        ksum = None
        for i in range(pps):
            kpg = kp[i][...]
            page = s * pps + i
            lg_ref[page] = lax.dot_general(qe, kpg.astype(BF16), NT_DIMS, preferred_element_type=F32)
            part = jnp.sum(kpg, axis=0, keepdims=True)
            ksum = part if i % ppb == 0 else ksum + part
            if i % ppb == ppb - 1:
                km_ref[pl.ds(s * (pps // ppb) + i // ppb, 1), :] = ksum * (1.0 / MOBA_BLOCK)

    @pl.when(s == nsteps - 1)
    def _():
        qe_f = qe_ref[...]
        gate = lax.dot_general(qe_f, km_ref[...], NT_DIMS, precision=HIGHEST, preferred_element_type=F32)
        blk = lax.broadcasted_iota(jnp.int32, (rows, nblk), 1)
        bias = _topk_bias(gate, blk >= 0, blk.astype(F32))
        qe = (qe_f * ATT_SCALE).astype(BF16)
        lo = lax.dot_general(qe, kn_ref[0].astype(BF16), NT_DIMS, preferred_element_type=F32)
        trow = lax.broadcasted_iota(jnp.int32, lo.shape, 0) // ATT_HEADS
        tcol = lax.broadcasted_iota(jnp.int32, lo.shape, 1)
        lo = jnp.where((tcol <= trow) & (tcol < steps), lo, NEG)
        m = jnp.max(lo, axis=-1, keepdims=True)
        for b in range(nblk):
            for pg in range(ppb):
                sb = lg_ref[b * ppb + pg] + bias[:, b:b + 1]
                m = jnp.maximum(m, jnp.max(sb, axis=-1, keepdims=True))
        po = jnp.exp(lo - m)
        lsum = jnp.sum(po, axis=-1, keepdims=True)
        for b in range(nblk):
            for pg in range(ppb):
                pb = jnp.exp(lg_ref[b * ppb + pg] + bias[:, b:b + 1] - m)
                lsum = lsum + jnp.sum(pb, axis=-1, keepdims=True)
                p_ref[b * ppb + pg] = pb.astype(BF16)
        lsum_ref[...] = lsum
        acc_ref[...] = jnp.dot(po.astype(BF16), vn_ref[0].astype(BF16), preferred_element_type=F32)

    @pl.when(s >= nsteps)
    def _():
        acc = acc_ref[...]
        for i in range(pps):
            page = (s - nsteps) * pps + i
            acc = acc + jnp.dot(p_ref[page], vp[i][...].astype(BF16), preferred_element_type=F32)
        acc_ref[...] = acc

    @pl.when(s == 2 * nsteps - 1)
    def _():
        r = acc_ref[...] / lsum_ref[...]
        outs = [jnp.sum(jnp.where(head_mask, r[t * ATT_HEADS:(t + 1) * ATT_HEADS, :], 0.0),
                        axis=0, keepdims=True) for t in range(steps)]
        outs.append(jnp.zeros((SUBLANES - steps, ATT_WIDTH), F32))
        o_ref[0] = jnp.concatenate(outs, axis=0)


def _moba_sample_call(page_flat, q8, kn8, vn8, cache_k4, cache_v4, layer, npages, steps):
    batch = q8.shape[0]
    pps = PAGES_PER_STEP
    nsteps = npages // pps
    rows = steps * ATT_HEADS
    tok = pl.BlockSpec((1, SUBLANES, ATT_WIDTH), lambda n_, s_, pt: (n_, 0, 0))

    def kspec(i):
        return pl.BlockSpec(
            (None, None, PAGE_SIZE, ATT_WIDTH),
            lambda n_, s_, pt: (layer, pt[n_ * npages + jnp.minimum(s_, nsteps - 1) * pps + i], 0, 0))

    def vspec(i):
        return pl.BlockSpec(
            (None, None, PAGE_SIZE, ATT_WIDTH),
            lambda n_, s_, pt: (layer, pt[n_ * npages + jnp.maximum(s_ - nsteps, 0) * pps + i], 0, 0))

    grid_spec = pltpu.PrefetchScalarGridSpec(
        num_scalar_prefetch=1,
        grid=(batch, 2 * nsteps),
        in_specs=[tok, tok, tok] + [kspec(i) for i in range(pps)] + [vspec(i) for i in range(pps)],
        out_specs=tok,
        scratch_shapes=[
            pltpu.VMEM((npages, rows, PAGE_SIZE), F32),
            pltpu.VMEM((npages, rows, PAGE_SIZE), BF16),
            pltpu.VMEM((npages * PAGE_SIZE // MOBA_BLOCK, ATT_WIDTH), F32),
            pltpu.VMEM((rows, ATT_WIDTH), F32),
            pltpu.VMEM((rows, 1), F32),
            pltpu.VMEM((rows, ATT_WIDTH), F32),
        ],
    )
    return pl.pallas_call(
        functools.partial(_moba_sample_kernel, npages=npages),
        grid_spec=grid_spec,
        out_shape=jax.ShapeDtypeStruct((batch, SUBLANES, ATT_WIDTH), F32),
        compiler_params=_cparams(("arbitrary", "arbitrary")),
        name="moba_sample",
    )(page_flat, q8, kn8, vn8, *([cache_k4] * pps), *([cache_v4] * pps))


def _gdn_kernel(q_ref, k_ref, v_ref, gb_ref, s0_ref, o_ref, sout_ref, s_scr, *, chunk):
    c = pl.program_id(1)
    nc = pl.num_programs(1)

    @pl.when(c == 0)
    def _():
        s_scr[...] = s0_ref[0]

    gb = gb_ref[0]
    ri = lax.broadcasted_iota(jnp.int32, (chunk, chunk), 0)
    ci = lax.broadcasted_iota(jnp.int32, (chunk, chunk), 1)
    tril = ri >= ci
    stril = ri > ci
    eye = (ri == ci).astype(F32)
    lane = lax.broadcasted_iota(jnp.int32, (1, LANES), 1)
    gcum_all = jnp.dot(tril.astype(F32), gb, precision=HIGHEST, preferred_element_type=F32)
    nsq = int(round(math.log2(chunk))) - 1

    def mm(a, b):
        return jnp.dot(a, b, precision=HIGHEST, preferred_element_type=F32)

    for h in range(DN_HEADS):
        cols = slice(h * DN_HEAD_DIM, (h + 1) * DN_HEAD_DIM)
        q = q_ref[0][:, cols]
        k = k_ref[0][:, cols]
        v = v_ref[0][:, cols]
        beta = gb[:, h:h + 1]
        gc = gcum_all[:, DN_HEADS + h:DN_HEADS + h + 1]
        a1 = jnp.where(lane == 0, gc, jnp.where(lane == 1, 1.0, 0.0))
        a2 = jnp.where(lane == 0, 1.0, jnp.where(lane == 1, -gc, 0.0))
        diff = lax.dot_general(a1, a2, NT_DIMS, precision=HIGHEST, preferred_element_type=F32)
        decay = jnp.where(tril, jnp.exp(jnp.where(tril, diff, 0.0)), 0.0)
        kbeta = k * beta
        kk = lax.dot_general(kbeta, k, NT_DIMS, precision=HIGHEST, preferred_element_type=F32)
        x = -jnp.where(stril, kk * decay, 0.0)
        tm = eye + x
        pw = x
        for _ in range(nsq):
            pw = mm(pw, pw)
            tm = tm + mm(tm, pw)
        eg = jnp.exp(gc)
        u = mm(tm, v * beta)
        w = mm(tm, kbeta * eg)
        qk = lax.dot_general(q, k, NT_DIMS, precision=HIGHEST, preferred_element_type=F32)
        attn = jnp.where(tril, qk * decay, 0.0)
        s_old = s_scr[h]
        v_new = u - mm(w, s_old)
        o_ref[0, :, cols] = mm(q * eg, s_old) + mm(attn, v_new)
        g_last = gc[chunk - 1:chunk, :]
        kdec = k * jnp.exp(g_last - gc)
        s_scr[h] = s_old * jnp.exp(g_last) + lax.dot_general(
            kdec, v_new, TN_DIMS, precision=HIGHEST, preferred_element_type=F32)

    @pl.when(c == nc - 1)
    def _():
        sout_ref[0] = s_scr[...]


def _gdn_call(qd, kd, vd, gb, s0, s0_index, chunk):
    n, t, _ = qd.shape
    nc = t // chunk
    row = lambda n_, c_: (n_, c_, 0)
    blk = pl.BlockSpec((1, chunk, DN_WIDTH), row)
    sshape = (DN_HEADS, DN_HEAD_DIM, DN_HEAD_DIM)
    s0_block = (None,) * (s0.ndim - 4) + (1,) + sshape
    return pl.pallas_call(
        functools.partial(_gdn_kernel, chunk=chunk),
        grid=(n, nc),
        in_specs=[blk, blk, blk,
                  pl.BlockSpec((1, chunk, LANES), row),
                  pl.BlockSpec(s0_block, lambda n_, c_: s0_index(n_))],
        out_specs=[blk, pl.BlockSpec((1,) + sshape, lambda n_, c_: (n_, 0, 0, 0))],
        out_shape=[jax.ShapeDtypeStruct((n, t, DN_WIDTH), F32),
                   jax.ShapeDtypeStruct((n,) + sshape, F32)],
        scratch_shapes=[pltpu.VMEM(sshape, F32)],
        compiler_params=_cparams(("arbitrary", "arbitrary")),
        name="gdn",
    )(qd, kd, vd, gb, s0)


def _out_kernel(oa_ref, sza_ref, od_ref, szd_ref, x_ref, gate_ref, wo_ref, nw_ref, lg_ref, lb_ref,
                o_ref, *, alpha):
    ya = (oa_ref[0] * sza_ref[0]).astype(BF16)
    od = od_ref[0]
    parts = []
    for h in range(DN_HEADS):
        oh = od[:, h * DN_HEAD_DIM:(h + 1) * DN_HEAD_DIM]
        ms = jnp.mean(oh * oh, axis=-1, keepdims=True)
        parts.append(oh * lax.rsqrt(ms + RMS_EPS) * nw_ref[...])
    yd = (jnp.concatenate(parts, axis=-1) * szd_ref[0]).astype(BF16)
    y = (jnp.dot(ya, wo_ref[0:ATT_WIDTH, :], preferred_element_type=F32)
         + jnp.dot(yd, wo_ref[ATT_WIDTH:ATT_WIDTH + DN_WIDTH, :], preferred_element_type=F32))
    r = alpha * x_ref[0] + (1.0 + gate_ref[0]) * y
    mu = jnp.mean(r, axis=-1, keepdims=True)
    rc = r - mu
    var = jnp.mean(rc * rc, axis=-1, keepdims=True)
    o_ref[0] = rc * lax.rsqrt(var + LN_EPS) * lg_ref[...] + lb_ref[...]


def _out_call(oa, sza, od, szd, x, gate, w_out_bf, nw, lg, lb, tm, alpha):
    n, t, d = x.shape
    nt = t // tm
    row = lambda n_, t_: (n_, t_, 0)
    full = lambda n_, t_: (0, 0)
    blk = pl.BlockSpec((1, tm, ATT_WIDTH), row)
    if gate.shape[1] == 1:
        gate_spec = pl.BlockSpec((1, 1, d), lambda n_, t_: (n_, 0, 0))
    else:
        gate_spec = pl.BlockSpec((1, tm, d), row)
    return pl.pallas_call(
        functools.partial(_out_kernel, alpha=alpha),
        grid=(n, nt),
        in_specs=[blk, blk, blk, blk,
                  pl.BlockSpec((1, tm, d), row),
                  gate_spec,
                  pl.BlockSpec((ATT_WIDTH + DN_WIDTH, d), full),
                  pl.BlockSpec((1, DN_HEAD_DIM), full),
                  pl.BlockSpec((1, d), full),
                  pl.BlockSpec((1, d), full)],
        out_specs=pl.BlockSpec((1, tm, d), row),
        out_shape=jax.ShapeDtypeStruct((n, t, d), F32),
        compiler_params=_cparams(("arbitrary", "arbitrary")),
        name="out_proj",
    )(oa, sza, od, szd, x, gate, w_out_bf, nw, lg, lb)


def _rope_tables(pos):
    half = ATT_HEAD_DIM // 2
    inv_freq = ROPE_THETA ** (-jnp.arange(half, dtype=F32) / half)
    ang = pos.astype(F32)[:, None] * inv_freq[None, :]
    cos = jnp.cos(ang)
    sin = jnp.sin(ang)
    reps = LANES // ATT_HEAD_DIM
    return jnp.tile(cos, (1, 2 * reps)), jnp.tile(jnp.concatenate([-sin, sin], axis=1), (1, reps))


def _lane_row(vec, offset):
    return jnp.zeros((1, LANES), F32).at[0, offset:offset + vec.shape[0]].set(vec.astype(F32))


def kernel(x_prompt, x_sample, c_prompt, c_sample, cache_k, cache_v, page_table, state_ssm, state_conv,
           w_ada, b_ada, w_in, conv_w, a_log, dt_bias, dn_norm_w, w_out, ln_g, ln_b):
    depth, d, _ = w_ada.shape
    bp, seq, _ = x_prompt.shape
    bs, dec, _ = x_sample.shape
    npages = page_table.shape[1]
    past = npages * PAGE_SIZE
    n_pool = cache_k.shape[1]
    alpha = float((2 * depth) ** 0.25)
    tm = MOBA_BLOCK
    assert seq % tm == 0 and seq % DN_CHUNK == 0 and past % MOBA_BLOCK == 0
    assert dec <= SUBLANES and npages % PAGES_PER_STEP == 0

    nc_rows = bp + bs
    c_rows = -(-nc_rows // SUBLANES) * SUBLANES
    c_all = jnp.zeros((c_rows, d), F32).at[:bp].set(c_prompt).at[bp:nc_rows].set(c_sample)
    mod_all = _mod_call(c_all, w_ada, b_ada)

    cos_p, sin_p = _rope_tables(jnp.arange(seq, dtype=jnp.int32))
    pos_s = past + jnp.repeat(jnp.arange(dec, dtype=jnp.int32), bs)
    cos_s, sin_s = _rope_tables(pos_s)

    cache_k4 = cache_k.reshape(depth, n_pool, PAGE_SIZE, ATT_WIDTH)
    cache_v4 = cache_v.reshape(depth, n_pool, PAGE_SIZE, ATT_WIDTH)
    page_flat = page_table.reshape(-1).astype(jnp.int32)
    zero_state = jnp.zeros((bp, DN_HEADS, DN_HEAD_DIM, DN_HEAD_DIM), F32)

    def to_bm8(a):
        w = a.shape[-1]
        a = a.reshape(dec, bs, w).transpose(1, 0, 2)
        return jnp.pad(a, ((0, 0), (0, SUBLANES - dec), (0, 0)))

    xp = x_prompt
    xs_tm = x_sample.transpose(1, 0, 2).reshape(dec * bs, d)
    outs = [[] for _ in range(8)]
    for l in range(depth):
        w_main = w_in[l, :, :MAIN_COLS].astype(BF16)
        w_small = jnp.zeros((d, LANES), F32).at[:, :2 * DN_HEADS].set(w_in[l, :, MAIN_COLS:])
        alog = _lane_row(a_log[l], DN_HEADS)
        dtb = _lane_row(dt_bias[l], DN_HEADS)
        w_out_bf = w_out[l].astype(BF16)
        nw = dn_norm_w[l].reshape(1, DN_HEAD_DIM)
        lg = ln_g[l].reshape(1, d)
        lb = ln_b[l].reshape(1, d)
        mod_p = mod_all[l, :bp].reshape(bp, 1, 3 * d)
        mod_s = mod_all[l, bp:nc_rows]

        (q, k, v, k_bf, v_bf, kmean, sza, qd, kd, vd, szd, gb, nconv) = _in_prompt_call(
            xp, mod_p, w_main, w_small, conv_w[l], alog, dtb, cos_p, sin_p, tm)
        oa = _moba_prompt_call(q, k_bf, v_bf, kmean.reshape(bp, seq // tm, ATT_WIDTH))
        od, s_new = _gdn_call(qd, kd, vd, gb, zero_state, lambda n_: (n_, 0, 0, 0), DN_CHUNK)
        xp = _out_call(oa, sza, od, szd, xp, mod_p[:, :, 2 * d:], w_out_bf, nw, lg, lb, tm, alpha)
        outs[0].append(k.reshape(bp, seq, ATT_HEADS, ATT_HEAD_DIM))
        outs[1].append(v.reshape(bp, seq, ATT_HEADS, ATT_HEAD_DIM))
        outs[2].append(s_new)
        outs[3].append(nconv[:, SUBLANES - (CONV_WIDTH - 1):, :])

        mod_rows = jnp.tile(mod_s, (dec, 1))
        hist_tm = state_conv[l].transpose(1, 0, 2).reshape((CONV_WIDTH - 1) * bs, DN_QKV)
        (q, k, v, sza, qd, kd, vd, szd, gb, nconv) = _in_sample_call(
            xs_tm, mod_rows[:, :d], mod_rows[:, d:2 * d], w_main, w_small, conv_w[l], alog, dtb,
            cos_s, sin_s, hist_tm, bs, dec)
        k8, v8 = to_bm8(k), to_bm8(v)
        oa8 = _moba_sample_call(page_flat, to_bm8(q), k8, v8, cache_k4, cache_v4, l, npages, dec)
        od8, s_new = _gdn_call(to_bm8(qd), to_bm8(kd), to_bm8(vd), to_bm8(gb), state_ssm,
                               lambda n_, l=l: (l, n_, 0, 0, 0), SUBLANES)
        rows8 = bs * SUBLANES
        flat = lambda a: a.reshape(1, rows8, a.shape[-1])
        gate_rows = jnp.repeat(mod_s[:, 2 * d:], SUBLANES, axis=0).reshape(1, rows8, d)
        xs8 = _out_call(flat(oa8), flat(to_bm8(sza)), flat(od8), flat(to_bm8(szd)), flat(to_bm8(xs_tm)),
                        gate_rows, w_out_bf, nw, lg, lb, rows8, alpha)
        xs_bm = xs8.reshape(bs, SUBLANES, d)[:, :dec]
        xs_tm = xs_bm.transpose(1, 0, 2).reshape(dec * bs, d)
        outs[4].append(k8[:, :dec].reshape(bs, dec, ATT_HEADS, ATT_HEAD_DIM))
        outs[5].append(v8[:, :dec].reshape(bs, dec, ATT_HEADS, ATT_HEAD_DIM))
        outs[6].append(s_new)
        outs[7].append(nconv.reshape(CONV_WIDTH - 1, bs, DN_QKV).transpose(1, 0, 2))

    stacked = [jnp.stack(o) for o in outs]
    return (xp, xs_bm, stacked[0], stacked[1], stacked[2], stacked[3],
            stacked[4], stacked[5], stacked[6], stacked[7])
```

```python
import functools
import math

import jax
import jax.numpy as jnp
from jax import lax
from jax.experimental import pallas as pl
from jax.experimental.pallas import tpu as pltpu

F32 = jnp.float32
BF16 = jnp.bfloat16
HIGHEST = lax.Precision.HIGHEST

ATT_HEADS = 8
ATT_HEAD_DIM = 64
ATT_WIDTH = ATT_HEADS * ATT_HEAD_DIM
DN_HEADS = 4
DN_HEAD_DIM = 128
DN_WIDTH = DN_HEADS * DN_HEAD_DIM
DN_QKV = 3 * DN_WIDTH
MAIN_COLS = 4 * ATT_WIDTH + DN_QKV + DN_WIDTH
MOBA_BLOCK = 256
MOBA_TOPK = 3
DN_CHUNK = 64
CONV_WIDTH = 4
PAGE_SIZE = 128
ROPE_THETA = 10000.0
LN_EPS = 1e-5
RMS_EPS = 1e-6
L2_EPS = 1e-6
ATT_SCALE = ATT_HEAD_DIM ** -0.5
NEG = -1e30
LANES = 128
SUBLANES = 8
VT_ROWS = ATT_HEAD_DIM + 16
MAX_MOBA_BLOCKS = LANES - ATT_HEAD_DIM
PAGES_PER_STEP = 8
VMEM_LIMIT = 56 * 1024 * 1024

NN_DIMS = (((1,), (0,)), ((), ()))
NT_DIMS = (((1,), (1,)), ((), ()))
TN_DIMS = (((0,), (0,)), ((), ()))


def _sigmoid(x):
    return 1.0 / (1.0 + jnp.exp(-x))


def _silu(x):
    return x * _sigmoid(x)


def _softplus(x):
    return jnp.maximum(x, 0.0) + jnp.log1p(jnp.exp(-jnp.abs(x)))


def _split_bf16(a):
    hi = a.astype(BF16)
    return hi, (a - hi.astype(F32)).astype(BF16)


def _mm3(a, b, dims=NN_DIMS):
    ah, al = _split_bf16(a)
    bh, bl = _split_bf16(b)

    def dot(x, y):
        return lax.dot_general(x, y, dims, preferred_element_type=F32)

    return (dot(ah, bl) + dot(al, bh)) + dot(ah, bh)


def _cparams(sem):
    return pltpu.CompilerParams(dimension_semantics=sem, vmem_limit_bytes=VMEM_LIMIT)


def _mod_kernel(c_ref, w_ref, b_ref, o_ref):
    s = _silu(c_ref[...])
    o_ref[0] = jnp.dot(s.astype(BF16), w_ref[0].astype(BF16), preferred_element_type=F32) + b_ref[0]


def _mod_call(c_all, w_ada, b_ada):
    depth, d, d3 = w_ada.shape
    rows = c_all.shape[0]
    nj = d3 // d
    return pl.pallas_call(
        _mod_kernel,
        grid=(depth, nj),
        in_specs=[
            pl.BlockSpec((rows, d), lambda l, j: (0, 0)),
            pl.BlockSpec((1, d, d), lambda l, j: (l, 0, j)),
            pl.BlockSpec((1, 1, d), lambda l, j: (l, 0, j)),
        ],
        out_specs=pl.BlockSpec((1, rows, d), lambda l, j: (l, 0, j)),
        out_shape=jax.ShapeDtypeStruct((depth, rows, d3), F32),
        compiler_params=_cparams(("arbitrary", "arbitrary")),
        name="mod",
    )(c_all, w_ada, b_ada.reshape(depth, 1, d3))


def _rope(p, cos, sin_signed):
    lane = lax.broadcasted_iota(jnp.int32, (1, LANES), 1)
    first_half = (lane % ATT_HEAD_DIM) < (ATT_HEAD_DIM // 2)
    outs = []
    for j in range(ATT_WIDTH // LANES):
        xj = p[:, j * LANES:(j + 1) * LANES]
        partner = jnp.where(first_half, pltpu.roll(xj, LANES - 32, 1), pltpu.roll(xj, 32, 1))
        outs.append(xj * cos + partner * sin_signed)
    return jnp.concatenate(outs, axis=-1)


def _l2norm_heads(c, scale):
    outs = []
    for h in range(DN_HEADS):
        ch = c[:, h * DN_HEAD_DIM:(h + 1) * DN_HEAD_DIM]
        ss = jnp.sum(ch * ch, axis=-1, keepdims=True)
        outs.append(ch * (lax.rsqrt(ss + L2_EPS) * scale))
    return jnp.concatenate(outs, axis=-1)


def _beta_decay(h, ws_ref, alog_ref, dtb_ref):
    raw = jnp.dot(h, ws_ref[...], precision=HIGHEST, preferred_element_type=F32)
    lane = lax.broadcasted_iota(jnp.int32, (1, LANES), 1)
    beta = _sigmoid(raw)
    g = -jnp.exp(alog_ref[...]) * _softplus(raw + dtb_ref[...])
    return jnp.where(lane < DN_HEADS, beta, jnp.where(lane < 2 * DN_HEADS, g, 0.0))


def _proj(hb, w_ref, g):
    return jnp.dot(hb, w_ref[:, g * ATT_WIDTH:(g + 1) * ATT_WIDTH], preferred_element_type=F32)


def _head_slabs(p, fill):
    lane = lax.broadcasted_iota(jnp.int32, (1, LANES), 1)
    low = lane < ATT_HEAD_DIM
    outs = []
    for j in range(ATT_WIDTH // LANES):
        xj = p[:, j * LANES:(j + 1) * LANES]
        outs.append(jnp.where(low, xj, fill))
        outs.append(jnp.where(low, pltpu.roll(xj, ATT_HEAD_DIM, 1), fill))
    return jnp.concatenate(outs, axis=-1)


def _in_prompt_kernel(x_ref, mod_ref, w_ref, ws_ref, cw_ref, alog_ref, dtb_ref, cos_ref, sin_ref,
                      qa_ref, k_ref, v_ref, ka_ref, vt_ref, km_ref, sza_ref,
                      qd_ref, kd_ref, vd_ref, szd_ref, gb_ref, nc_ref, xbuf, *, tm):
    t = pl.program_id(1)
    d = x_ref.shape[-1]
    hist = SUBLANES

    @pl.when(t == 0)
    def _():
        xbuf[0:hist, :] = jnp.zeros((hist, DN_QKV), F32)

    x = x_ref[0]
    shift = mod_ref[0][:, 0:d]
    scale = mod_ref[0][:, d:2 * d]
    h = x * (1.0 + scale) + shift
    hb = h.astype(BF16)
    cos = cos_ref[...]
    sin = sin_ref[...]

    lane = lax.broadcasted_iota(jnp.int32, (1, LANES), 1)
    qa_ref[0] = _head_slabs(_rope(_proj(hb, w_ref, 0), cos, sin) * ATT_SCALE, 0.0)
    kr = _rope(_proj(hb, w_ref, 1), cos, sin)
    k_ref[0] = kr
    block_onehot = (lane == ATT_HEAD_DIM + t).astype(F32)
    ka_ref[0] = _head_slabs(kr, block_onehot).astype(BF16)
    km_ref[0, 0] = jnp.sum(kr, axis=0, keepdims=True) * (1.0 / tm)
    v = _proj(hb, w_ref, 2)
    v_ref[0] = v
    vt = v.T
    for hd in range(ATT_HEADS):
        vt_ref[0, 0, hd, 0:ATT_HEAD_DIM, :] = vt[hd * ATT_HEAD_DIM:(hd + 1) * ATT_HEAD_DIM, :].astype(BF16)
        vt_ref[0, 0, hd, ATT_HEAD_DIM:VT_ROWS, :] = jnp.ones((VT_ROWS - ATT_HEAD_DIM, tm), BF16)
    sza_ref[0] = _silu(_proj(hb, w_ref, 3))
    szd_ref[0] = _silu(_proj(hb, w_ref, 7))
    gb_ref[0] = _beta_decay(h, ws_ref, alog_ref, dtb_ref)

    outs = (qd_ref, kd_ref, vd_ref)
    for gi in range(3):
        cols = slice(gi * DN_WIDTH, (gi + 1) * DN_WIDTH)
        xbuf[hist:hist + tm, cols] = _proj(hb, w_ref, 4 + gi)
        conv = jnp.zeros((tm, DN_WIDTH), F32)
        for i in range(CONV_WIDTH):
            off = hist - (CONV_WIDTH - 1) + i
            conv = conv + xbuf[off:off + tm, cols] * cw_ref[i:i + 1, cols]
        c = _silu(conv)
        if gi == 0:
            c = _l2norm_heads(c, DN_HEAD_DIM ** -0.5)
        elif gi == 1:
            c = _l2norm_heads(c, 1.0)
        outs[gi][0] = c

    tail = xbuf[tm:tm + hist, :]
    nc_ref[0] = tail
    xbuf[0:hist, :] = tail


def _in_prompt_call(x, mod, w_main, w_small, conv_w, alog, dtb, cos, sin, tm):
    n, t, d = x.shape
    nt = t // tm
    row = lambda n_, t_: (n_, t_, 0)
    full = lambda n_, t_: (0, 0)
    wide = jax.ShapeDtypeStruct((n, t, ATT_WIDTH), F32)
    blk = pl.BlockSpec((1, tm, ATT_WIDTH), row)
    slab_w = ATT_HEADS * LANES
    slab_blk = pl.BlockSpec((1, tm, slab_w), row)
    return pl.pallas_call(
        functools.partial(_in_prompt_kernel, tm=tm),
        grid=(n, nt),
        in_specs=[
            pl.BlockSpec((1, tm, d), row),
            pl.BlockSpec((1, 1, 3 * d), lambda n_, t_: (n_, 0, 0)),
            pl.BlockSpec((d, MAIN_COLS), full),
            pl.BlockSpec((d, LANES), full),
            pl.BlockSpec((CONV_WIDTH, DN_QKV), full),
            pl.BlockSpec((1, LANES), full),
            pl.BlockSpec((1, LANES), full),
            pl.BlockSpec((tm, LANES), lambda n_, t_: (t_, 0)),
            pl.BlockSpec((tm, LANES), lambda n_, t_: (t_, 0)),
        ],
        out_specs=[
            slab_blk, blk, blk, slab_blk,
            pl.BlockSpec((1, 1, ATT_HEADS, VT_ROWS, tm), lambda n_, t_: (n_, t_, 0, 0, 0)),
            pl.BlockSpec((1, 1, 1, ATT_WIDTH), lambda n_, t_: (n_, t_, 0, 0)),
            blk, blk, blk, blk, blk,
            pl.BlockSpec((1, tm, LANES), row),
            pl.BlockSpec((1, SUBLANES, DN_QKV), lambda n_, t_: (n_, 0, 0)),
        ],
        out_shape=[
            jax.ShapeDtypeStruct((n, t, slab_w), F32), wide, wide,
            jax.ShapeDtypeStruct((n, t, slab_w), BF16),
            jax.ShapeDtypeStruct((n, nt, ATT_HEADS, VT_ROWS, tm), BF16),
            jax.ShapeDtypeStruct((n, nt, 1, ATT_WIDTH), F32),
            wide, wide, wide, wide, wide,
            jax.ShapeDtypeStruct((n, t, LANES), F32),
            jax.ShapeDtypeStruct((n, SUBLANES, DN_QKV), F32),
        ],
        scratch_shapes=[pltpu.VMEM((tm + SUBLANES, DN_QKV), F32)],
        compiler_params=_cparams(("arbitrary", "arbitrary")),
        name="in_prompt",
    )(x, mod, w_main, w_small, conv_w, alog, dtb, cos, sin)


def _in_sample_kernel(x_ref, shift_ref, scale_ref, w_ref, ws_ref, cw_ref, alog_ref, dtb_ref,
                      cos_ref, sin_ref, hist_ref,
                      q_ref, k_ref, v_ref, sza_ref, qd_ref, kd_ref, vd_ref, szd_ref, gb_ref, nc_ref,
                      xall, *, batch, steps):
    rows = batch * steps
    nh = (CONV_WIDTH - 1) * batch
    h = x_ref[...] * (1.0 + scale_ref[...]) + shift_ref[...]
    hb = h.astype(BF16)
    cos = cos_ref[...]
    sin = sin_ref[...]
    q_ref[...] = _rope(_proj(hb, w_ref, 0), cos, sin)
    k_ref[...] = _rope(_proj(hb, w_ref, 1), cos, sin)
    v_ref[...] = _proj(hb, w_ref, 2)
    sza_ref[...] = _silu(_proj(hb, w_ref, 3))
    szd_ref[...] = _silu(_proj(hb, w_ref, 7))
    gb_ref[...] = _beta_decay(h, ws_ref, alog_ref, dtb_ref)

    xall[0:nh, :] = hist_ref[...]
    outs = (qd_ref, kd_ref, vd_ref)
    for gi in range(3):
        cols = slice(gi * DN_WIDTH, (gi + 1) * DN_WIDTH)
        xall[nh:nh + rows, cols] = _proj(hb, w_ref, 4 + gi)
        conv = jnp.zeros((rows, DN_WIDTH), F32)
        for i in range(CONV_WIDTH):
            conv = conv + xall[i * batch:i * batch + rows, cols] * cw_ref[i:i + 1, cols]
        c = _silu(conv)
        if gi == 0:
            c = _l2norm_heads(c, DN_HEAD_DIM ** -0.5)
        elif gi == 1:
            c = _l2norm_heads(c, 1.0)
        outs[gi][...] = c
    nc_ref[...] = xall[rows:rows + nh, :]


def _in_sample_call(x_tm, shift, scale, w_main, w_small, conv_w, alog, dtb, cos, sin, hist_tm, batch, steps):
    rows = batch * steps
    nh = (CONV_WIDTH - 1) * batch
    wide = jax.ShapeDtypeStruct((rows, ATT_WIDTH), F32)
    return pl.pallas_call(
        functools.partial(_in_sample_kernel, batch=batch, steps=steps),
        out_shape=[wide] * 8 + [jax.ShapeDtypeStruct((rows, LANES), F32),
                                jax.ShapeDtypeStruct((nh, DN_QKV), F32)],
        scratch_shapes=[pltpu.VMEM((rows + nh, DN_QKV), F32)],
        compiler_params=pltpu.CompilerParams(vmem_limit_bytes=VMEM_LIMIT),
        name="in_sample",
    )(x_tm, shift, scale, w_main, w_small, conv_w, alog, dtb, cos, sin, hist_tm)


def _topk_bias(gate, valid, blkf, axis=-1):
    gate = jnp.where(valid, gate, -jnp.inf)
    bias = jnp.full(gate.shape, NEG, F32)
    for _ in range(MOBA_TOPK):
        mx = jnp.max(gate, axis=axis, keepdims=True)
        cand = (gate == mx) & (mx > -jnp.inf)
        idx = jnp.min(jnp.where(cand, blkf, 1e9), axis=axis, keepdims=True)
        pick = blkf == idx
        bias = jnp.where(pick, 0.0, bias)
        gate = jnp.where(pick, -jnp.inf, gate)
    return bias


def _moba_prompt_kernel(q_ref, k_ref, vt_ref, km_ref, o_ref):
    i = pl.program_id(2)
    bs = MOBA_BLOCK
    lane = lax.broadcasted_iota(jnp.int32, (1, LANES), 1)
    low = lane < ATT_HEAD_DIM
    brow = lax.broadcasted_iota(jnp.int32, (MAX_MOBA_BLOCKS, bs), 0)
    browf = brow.astype(F32)
    krow = lax.broadcasted_iota(jnp.int32, (bs, bs), 0)
    qcol = lax.broadcasted_iota(jnp.int32, (bs, bs), 1)

    def pv(j, hh, p):
        return jnp.dot(vt_ref[0, j, hh], p.astype(BF16), preferred_element_type=F32)

    qas = []
    for hh in range(2):
        qs = q_ref[0][:, hh * LANES:(hh + 1) * LANES]
        gate_t = lax.dot_general(km_ref[0, hh], qs, NT_DIMS, precision=HIGHEST,
                                 preferred_element_type=F32)
        bias = _topk_bias(gate_t[ATT_HEAD_DIM:, :], brow < i, browf, axis=0)
        bias = jnp.where(brow == i, 0.0, bias)
        bias_t = jnp.concatenate([jnp.full((ATT_HEAD_DIM, bs), NEG, F32), bias], axis=0)
        qas.append(jnp.where(low, qs, bias_t.T).astype(BF16))

    def scores(j):
        off = pl.multiple_of(j * bs, bs)
        return [lax.dot_general(k_ref[0, pl.ds(off, bs), hh * LANES:(hh + 1) * LANES], qas[hh],
                                NT_DIMS, preferred_element_type=F32) for hh in range(2)]

    init = []
    for hh, s in enumerate(scores(i)):
        s = jnp.where(krow <= qcol, s, NEG)
        m = jnp.max(s, axis=0, keepdims=True)
        init += [m, pv(i, hh, jnp.exp(s - m))]
    init += scores(jnp.maximum(i - 1, 0))

    def body(t, carry):
        j = i - 1 - t
        nxt = scores(jnp.maximum(j - 1, 0))
        new = []
        for hh in range(2):
            m, acc, s = carry[2 * hh], carry[2 * hh + 1], carry[4 + hh]
            m_new = jnp.maximum(m, jnp.max(s, axis=0, keepdims=True))
            acc = jnp.exp(m - m_new) * acc + pv(j, hh, jnp.exp(s - m_new))
            new += [m_new, acc]
        return tuple(new + nxt)

    _, a0, _, a1, _, _ = lax.fori_loop(0, i, body, tuple(init))
    d = ATT_HEAD_DIM
    o = jnp.concatenate([a0[0:d] / a0[d:d + 1], a1[0:d] / a1[d:d + 1]], axis=0)
    o_ref[0] = o.T


def _moba_prompt_call(q_aug, k_aug, vt, km_pad):
    n, t, _ = q_aug.shape
    nb = t // MOBA_BLOCK
    npair = ATT_WIDTH // LANES
    pair_w = 2 * LANES
    return pl.pallas_call(
        _moba_prompt_kernel,
        grid=(n, npair, nb),
        in_specs=[
            pl.BlockSpec((1, MOBA_BLOCK, pair_w), lambda n_, p_, i_: (n_, i_, p_)),
            pl.BlockSpec((1, t, pair_w), lambda n_, p_, i_: (n_, 0, p_)),
            pl.BlockSpec((1, nb, 2, VT_ROWS, MOBA_BLOCK), lambda n_, p_, i_: (n_, 0, p_, 0, 0)),
            pl.BlockSpec((1, 2, LANES, LANES), lambda n_, p_, i_: (n_, p_, 0, 0)),
        ],
        out_specs=pl.BlockSpec((1, MOBA_BLOCK, LANES), lambda n_, p_, i_: (n_, i_, p_)),
        out_shape=jax.ShapeDtypeStruct((n, t, ATT_WIDTH), F32),
        compiler_params=_cparams(("arbitrary", "arbitrary", "arbitrary")),
        name="moba_prompt",
    )(q_aug, k_aug, vt, km_pad)


def _moba_sample_kernel(pt_ref, q_ref, kn_ref, vn_ref, *rest, npages):
    pps = PAGES_PER_STEP
    kp = rest[:pps]
    vp = rest[pps:2 * pps]
    o_ref = rest[2 * pps]
    lg_ref, p_ref, km_ref, acc_ref, lsum_ref, qe_ref = rest[2 * pps + 1:]
    s = pl.program_id(1)
    nsteps = npages // pps
    nblk = npages * PAGE_SIZE // MOBA_BLOCK
    ppb = MOBA_BLOCK // PAGE_SIZE
    rows = qe_ref.shape[0]
    steps = rows // ATT_HEADS
    lane = lax.broadcasted_iota(jnp.int32, (ATT_HEADS, ATT_WIDTH), 1)
    hrow = lax.broadcasted_iota(jnp.int32, (ATT_HEADS, ATT_WIDTH), 0)
    head_mask = (lane // ATT_HEAD_DIM) == hrow

    @pl.when(s == 0)
    def _():
        q = q_ref[0]
        for t in range(steps):
            qe_ref[t * ATT_HEADS:(t + 1) * ATT_HEADS, :] = jnp.where(head_mask, q[t:t + 1, :], 0.0)

    @pl.when(s < nsteps)
    def _():
        qe = (qe_ref[...] * ATT_SCALE).astype(BF16)
        ksum = None
        for i in range(pps):
            kpg = kp[i][...]
            page = s * pps + i
            lg_ref[page] = lax.dot_general(qe, kpg.astype(BF16), NT_DIMS, preferred_element_type=F32)
            part = jnp.sum(kpg, axis=0, keepdims=True)
            ksum = part if i % ppb == 0 else ksum + part
            if i % ppb == ppb - 1:
                km_ref[pl.ds(s * (pps // ppb) + i // ppb, 1), :] = ksum * (1.0 / MOBA_BLOCK)

    @pl.when(s == nsteps - 1)
    def _():
        qe_f = qe_ref[...]
        gate = lax.dot_general(qe_f, km_ref[...], NT_DIMS, precision=HIGHEST, preferred_element_type=F32)
        blk = lax.broadcasted_iota(jnp.int32, (rows, nblk), 1)
        bias = _topk_bias(gate, blk >= 0, blk.astype(F32))
        qe = (qe_f * ATT_SCALE).astype(BF16)
        lo = lax.dot_general(qe, kn_ref[0].astype(BF16), NT_DIMS, preferred_element_type=F32)
        trow = lax.broadcasted_iota(jnp.int32, lo.shape, 0) // ATT_HEADS
        tcol = lax.broadcasted_iota(jnp.int32, lo.shape, 1)
        lo = jnp.where((tcol <= trow) & (tcol < steps), lo, NEG)
        m = jnp.max(lo, axis=-1, keepdims=True)
        for b in range(nblk):
            for pg in range(ppb):
                sb = lg_ref[b * ppb + pg] + bias[:, b:b + 1]
                m = jnp.maximum(m, jnp.max(sb, axis=-1, keepdims=True))
        po = jnp.exp(lo - m)
        lsum = jnp.sum(po, axis=-1, keepdims=True)
        for b in range(nblk):
            for pg in range(ppb):
                pb = jnp.exp(lg_ref[b * ppb + pg] + bias[:, b:b + 1] - m)
                lsum = lsum + jnp.sum(pb, axis=-1, keepdims=True)
                p_ref[b * ppb + pg] = pb.astype(BF16)
        lsum_ref[...] = lsum
        acc_ref[...] = jnp.dot(po.astype(BF16), vn_ref[0].astype(BF16), preferred_element_type=F32)

    @pl.when(s >= nsteps)
    def _():
        acc = acc_ref[...]
        for i in range(pps):
            page = (s - nsteps) * pps + i
            acc = acc + jnp.dot(p_ref[page], vp[i][...].astype(BF16), preferred_element_type=F32)
        acc_ref[...] = acc

    @pl.when(s == 2 * nsteps - 1)
    def _():
        r = acc_ref[...] / lsum_ref[...]
        outs = [jnp.sum(jnp.where(head_mask, r[t * ATT_HEADS:(t + 1) * ATT_HEADS, :], 0.0),
                        axis=0, keepdims=True) for t in range(steps)]
        outs.append(jnp.zeros((SUBLANES - steps, ATT_WIDTH), F32))
        o_ref[0] = jnp.concatenate(outs, axis=0)


def _moba_sample_call(page_flat, q8, kn8, vn8, cache_k4, cache_v4, layer, npages, steps):
    batch = q8.shape[0]
    pps = PAGES_PER_STEP
    nsteps = npages // pps
    rows = steps * ATT_HEADS
    tok = pl.BlockSpec((1, SUBLANES, ATT_WIDTH), lambda n_, s_, pt: (n_, 0, 0))

    def kspec(i):
        return pl.BlockSpec(
            (None, None, PAGE_SIZE, ATT_WIDTH),
            lambda n_, s_, pt: (layer, pt[n_ * npages + jnp.minimum(s_, nsteps - 1) * pps + i], 0, 0))

    def vspec(i):
        return pl.BlockSpec(
            (None, None, PAGE_SIZE, ATT_WIDTH),
            lambda n_, s_, pt: (layer, pt[n_ * npages + jnp.maximum(s_ - nsteps, 0) * pps + i], 0, 0))

    grid_spec = pltpu.PrefetchScalarGridSpec(
        num_scalar_prefetch=1,
        grid=(batch, 2 * nsteps),
        in_specs=[tok, tok, tok] + [kspec(i) for i in range(pps)] + [vspec(i) for i in range(pps)],
        out_specs=tok,
        scratch_shapes=[
            pltpu.VMEM((npages, rows, PAGE_SIZE), F32),
            pltpu.VMEM((npages, rows, PAGE_SIZE), BF16),
            pltpu.VMEM((npages * PAGE_SIZE // MOBA_BLOCK, ATT_WIDTH), F32),
            pltpu.VMEM((rows, ATT_WIDTH), F32),
            pltpu.VMEM((rows, 1), F32),
            pltpu.VMEM((rows, ATT_WIDTH), F32),
        ],
    )
    return pl.pallas_call(
        functools.partial(_moba_sample_kernel, npages=npages),
        grid_spec=grid_spec,
        out_shape=jax.ShapeDtypeStruct((batch, SUBLANES, ATT_WIDTH), F32),
        compiler_params=_cparams(("arbitrary", "arbitrary")),
        name="moba_sample",
    )(page_flat, q8, kn8, vn8, *([cache_k4] * pps), *([cache_v4] * pps))


def _gdn_kernel(q_ref, k_ref, v_ref, gb_ref, s0_ref, o_ref, sout_ref, s_scr, *, chunk):
    c = pl.program_id(1)
    nc = pl.num_programs(1)

    @pl.when(c == 0)
    def _():
        s_scr[...] = s0_ref[0]

    gb = gb_ref[0]
    ri = lax.broadcasted_iota(jnp.int32, (chunk, chunk), 0)
    ci = lax.broadcasted_iota(jnp.int32, (chunk, chunk), 1)
    tril = ri >= ci
    stril = ri > ci
    eye = (ri == ci).astype(F32)
    lane = lax.broadcasted_iota(jnp.int32, (1, LANES), 1)
    gcum_all = jnp.dot(tril.astype(F32), gb, precision=HIGHEST, preferred_element_type=F32)
    gcum_rows = lax.dot_general(gcum_all, eye, TN_DIMS, precision=HIGHEST, preferred_element_type=F32)
    nsq = int(round(math.log2(chunk))) - 1

    def mm1(a, b, dims=NN_DIMS):
        return lax.dot_general(a.astype(BF16), b.astype(BF16), dims, preferred_element_type=F32)

    hs = range(DN_HEADS)
    cols = [slice(h * DN_HEAD_DIM, (h + 1) * DN_HEAD_DIM) for h in hs]
    q = [q_ref[0][:, cols[h]] for h in hs]
    k = [k_ref[0][:, cols[h]] for h in hs]
    v = [v_ref[0][:, cols[h]] for h in hs]
    beta = [gb[:, h:h + 1] for h in hs]
    gc = [gcum_all[:, DN_HEADS + h:DN_HEADS + h + 1] for h in hs]
    kbeta = [k[h] * beta[h] for h in hs]
    kk = [_mm3(kbeta[h], k[h], NT_DIMS) for h in hs]
    qk = [mm1(q[h], k[h], NT_DIMS) for h in hs]
    decay = []
    for h in hs:
        diff = gc[h] - gcum_rows[DN_HEADS + h:DN_HEADS + h + 1, :]
        decay.append(jnp.where(tril, jnp.exp(jnp.where(tril, diff, 0.0)), 0.0))
    eg = [jnp.exp(gc[h]) for h in hs]
    pw = [-jnp.where(stril, kk[h] * decay[h], 0.0) for h in hs]
    y = [jnp.concatenate([v[h] * beta[h], kbeta[h] * eg[h]], axis=-1) for h in hs]
    y = [y[h] + _mm3(pw[h], y[h]) for h in hs]
    for _ in range(nsq):
        pw = [_mm3(pw[h], pw[h]) for h in hs]
        y = [y[h] + _mm3(pw[h], y[h]) for h in hs]
    s_old = [s_scr[h] for h in hs]
    v_new = [y[h][:, 0:DN_HEAD_DIM] - _mm3(y[h][:, DN_HEAD_DIM:2 * DN_HEAD_DIM], s_old[h]) for h in hs]
    g_last = [gc[h][chunk - 1:chunk, :] for h in hs]
    kv = [_mm3(k[h] * jnp.exp(g_last[h] - gc[h]), v_new[h], TN_DIMS) for h in hs]
    for h in hs:
        s_scr[h] = s_old[h] * jnp.exp(g_last[h]) + kv[h]
    for h in hs:
        attn = jnp.where(tril, qk[h] * decay[h], 0.0)
        o_ref[0, :, cols[h]] = mm1(q[h] * eg[h], s_old[h]) + mm1(attn, v_new[h])

    @pl.when(c == nc - 1)
    def _():
        sout_ref[0] = s_scr[...]


def _gdn_call(qd, kd, vd, gb, s0, s0_index, chunk):
    n, t, _ = qd.shape
    nc = t // chunk
    row = lambda n_, c_: (n_, c_, 0)
    blk = pl.BlockSpec((1, chunk, DN_WIDTH), row)
    sshape = (DN_HEADS, DN_HEAD_DIM, DN_HEAD_DIM)
    s0_block = (None,) * (s0.ndim - 4) + (1,) + sshape
    return pl.pallas_call(
        functools.partial(_gdn_kernel, chunk=chunk),
        grid=(n, nc),
        in_specs=[blk, blk, blk,
                  pl.BlockSpec((1, chunk, LANES), row),
                  pl.BlockSpec(s0_block, lambda n_, c_: s0_index(n_))],
        out_specs=[blk, pl.BlockSpec((1,) + sshape, lambda n_, c_: (n_, 0, 0, 0))],
        out_shape=[jax.ShapeDtypeStruct((n, t, DN_WIDTH), F32),
                   jax.ShapeDtypeStruct((n,) + sshape, F32)],
        scratch_shapes=[pltpu.VMEM(sshape, F32)],
        compiler_params=_cparams(("arbitrary", "arbitrary")),
        name="gdn",
    )(qd, kd, vd, gb, s0)


def _out_kernel(oa_ref, sza_ref, od_ref, szd_ref, x_ref, gate_ref, wo_ref, nw_ref, lg_ref, lb_ref,
                o_ref, *, alpha):
    ya = (oa_ref[0] * sza_ref[0]).astype(BF16)
    od = od_ref[0]
    parts = []
    for h in range(DN_HEADS):
        oh = od[:, h * DN_HEAD_DIM:(h + 1) * DN_HEAD_DIM]
        ms = jnp.mean(oh * oh, axis=-1, keepdims=True)
        parts.append(oh * lax.rsqrt(ms + RMS_EPS) * nw_ref[...])
    yd = (jnp.concatenate(parts, axis=-1) * szd_ref[0]).astype(BF16)
    y = (jnp.dot(ya, wo_ref[0:ATT_WIDTH, :], preferred_element_type=F32)
         + jnp.dot(yd, wo_ref[ATT_WIDTH:ATT_WIDTH + DN_WIDTH, :], preferred_element_type=F32))
    r = alpha * x_ref[0] + (1.0 + gate_ref[0]) * y
    mu = jnp.mean(r, axis=-1, keepdims=True)
    rc = r - mu
    var = jnp.mean(rc * rc, axis=-1, keepdims=True)
    o_ref[0] = rc * lax.rsqrt(var + LN_EPS) * lg_ref[...] + lb_ref[...]


def _out_call(oa, sza, od, szd, x, gate, w_out_bf, nw, lg, lb, tm, alpha):
    n, t, d = x.shape
    nt = t // tm
    row = lambda n_, t_: (n_, t_, 0)
    full = lambda n_, t_: (0, 0)
    blk = pl.BlockSpec((1, tm, ATT_WIDTH), row)
    if gate.shape[1] == 1:
        gate_spec = pl.BlockSpec((1, 1, d), lambda n_, t_: (n_, 0, 0))
    else:
        gate_spec = pl.BlockSpec((1, tm, d), row)
    return pl.pallas_call(
        functools.partial(_out_kernel, alpha=alpha),
        grid=(n, nt),
        in_specs=[blk, blk, blk, blk,
                  pl.BlockSpec((1, tm, d), row),
                  gate_spec,
                  pl.BlockSpec((ATT_WIDTH + DN_WIDTH, d), full),
                  pl.BlockSpec((1, DN_HEAD_DIM), full),
                  pl.BlockSpec((1, d), full),
                  pl.BlockSpec((1, d), full)],
        out_specs=pl.BlockSpec((1, tm, d), row),
        out_shape=jax.ShapeDtypeStruct((n, t, d), F32),
        compiler_params=_cparams(("arbitrary", "arbitrary")),
        name="out_proj",
    )(oa, sza, od, szd, x, gate, w_out_bf, nw, lg, lb)


def _rope_tables(pos):
    half = ATT_HEAD_DIM // 2
    inv_freq = ROPE_THETA ** (-jnp.arange(half, dtype=F32) / half)
    ang = pos.astype(F32)[:, None] * inv_freq[None, :]
    cos = jnp.cos(ang)
    sin = jnp.sin(ang)
    reps = LANES // ATT_HEAD_DIM
    return jnp.tile(cos, (1, 2 * reps)), jnp.tile(jnp.concatenate([-sin, sin], axis=1), (1, reps))


def _lane_row(vec, offset):
    return jnp.zeros((1, LANES), F32).at[0, offset:offset + vec.shape[0]].set(vec.astype(F32))


def kernel(x_prompt, x_sample, c_prompt, c_sample, cache_k, cache_v, page_table, state_ssm, state_conv,
           w_ada, b_ada, w_in, conv_w, a_log, dt_bias, dn_norm_w, w_out, ln_g, ln_b):
    depth, d, _ = w_ada.shape
    bp, seq, _ = x_prompt.shape
    bs, dec, _ = x_sample.shape
    npages = page_table.shape[1]
    past = npages * PAGE_SIZE
    n_pool = cache_k.shape[1]
    alpha = float((2 * depth) ** 0.25)
    tm = MOBA_BLOCK
    nb = seq // tm
    assert seq % tm == 0 and seq % DN_CHUNK == 0 and past % MOBA_BLOCK == 0 and nb <= MAX_MOBA_BLOCKS
    assert dec <= SUBLANES and npages % PAGES_PER_STEP == 0

    nc_rows = bp + bs
    c_rows = -(-nc_rows // SUBLANES) * SUBLANES
    c_all = jnp.zeros((c_rows, d), F32).at[:bp].set(c_prompt).at[bp:nc_rows].set(c_sample)
    mod_all = _mod_call(c_all, w_ada, b_ada)

    cos_p, sin_p = _rope_tables(jnp.arange(seq, dtype=jnp.int32))
    pos_s = past + jnp.repeat(jnp.arange(dec, dtype=jnp.int32), bs)
    cos_s, sin_s = _rope_tables(pos_s)

    cache_k4 = cache_k.reshape(depth, n_pool, PAGE_SIZE, ATT_WIDTH)
    cache_v4 = cache_v.reshape(depth, n_pool, PAGE_SIZE, ATT_WIDTH)
    page_flat = page_table.reshape(-1).astype(jnp.int32)
    zero_state = jnp.zeros((bp, DN_HEADS, DN_HEAD_DIM, DN_HEAD_DIM), F32)

    def to_bm8(a):
        w = a.shape[-1]
        a = a.reshape(dec, bs, w).transpose(1, 0, 2)
        return jnp.pad(a, ((0, 0), (0, SUBLANES - dec), (0, 0)))

    xp = x_prompt
    xs_tm = x_sample.transpose(1, 0, 2).reshape(dec * bs, d)
    outs = [[] for _ in range(8)]
    for l in range(depth):
        w_main = w_in[l, :, :MAIN_COLS].astype(BF16)
        w_small = jnp.zeros((d, LANES), F32).at[:, :2 * DN_HEADS].set(w_in[l, :, MAIN_COLS:])
        alog = _lane_row(a_log[l], DN_HEADS)
        dtb = _lane_row(dt_bias[l], DN_HEADS)
        w_out_bf = w_out[l].astype(BF16)
        nw = dn_norm_w[l].reshape(1, DN_HEAD_DIM)
        lg = ln_g[l].reshape(1, d)
        lb = ln_b[l].reshape(1, d)
        mod_p = mod_all[l, :bp].reshape(bp, 1, 3 * d)
        mod_s = mod_all[l, bp:nc_rows]

        (q_aug, k, v, k_aug, vt, kmean, sza, qd, kd, vd, szd, gb, nconv) = _in_prompt_call(
            xp, mod_p, w_main, w_small, conv_w[l], alog, dtb, cos_p, sin_p, tm)
        km_heads = kmean.reshape(bp, nb, ATT_HEADS, ATT_HEAD_DIM).transpose(0, 2, 1, 3)
        km_pad = jnp.pad(km_heads, ((0, 0), (0, 0), (ATT_HEAD_DIM, LANES - ATT_HEAD_DIM - nb),
                                    (0, LANES - ATT_HEAD_DIM)))
        oa = _moba_prompt_call(q_aug, k_aug, vt, km_pad)
        od, s_new = _gdn_call(qd, kd, vd, gb, zero_state, lambda n_: (n_, 0, 0, 0), DN_CHUNK)
        xp = _out_call(oa, sza, od, szd, xp, mod_p[:, :, 2 * d:], w_out_bf, nw, lg, lb, tm, alpha)
        outs[0].append(k.reshape(bp, seq, ATT_HEADS, ATT_HEAD_DIM))
        outs[1].append(v.reshape(bp, seq, ATT_HEADS, ATT_HEAD_DIM))
        outs[2].append(s_new)
        outs[3].append(nconv[:, SUBLANES - (CONV_WIDTH - 1):, :])

        mod_rows = jnp.tile(mod_s, (dec, 1))
        hist_tm = state_conv[l].transpose(1, 0, 2).reshape((CONV_WIDTH - 1) * bs, DN_QKV)
        (q, k, v, sza, qd, kd, vd, szd, gb, nconv) = _in_sample_call(
            xs_tm, mod_rows[:, :d], mod_rows[:, d:2 * d], w_main, w_small, conv_w[l], alog, dtb,
            cos_s, sin_s, hist_tm, bs, dec)
        k8, v8 = to_bm8(k), to_bm8(v)
        oa8 = _moba_sample_call(page_flat, to_bm8(q), k8, v8, cache_k4, cache_v4, l, npages, dec)
        od8, s_new = _gdn_call(to_bm8(qd), to_bm8(kd), to_bm8(vd), to_bm8(gb), state_ssm,
                               lambda n_, l=l: (l, n_, 0, 0, 0), SUBLANES)
        rows8 = bs * SUBLANES
        flat = lambda a: a.reshape(1, rows8, a.shape[-1])
        gate_rows = jnp.repeat(mod_s[:, 2 * d:], SUBLANES, axis=0).reshape(1, rows8, d)
        xs8 = _out_call(flat(oa8), flat(to_bm8(sza)), flat(od8), flat(to_bm8(szd)), flat(to_bm8(xs_tm)),
                        gate_rows, w_out_bf, nw, lg, lb, rows8, alpha)
        xs_bm = xs8.reshape(bs, SUBLANES, d)[:, :dec]
        xs_tm = xs_bm.transpose(1, 0, 2).reshape(dec * bs, d)
        outs[4].append(k8[:, :dec].reshape(bs, dec, ATT_HEADS, ATT_HEAD_DIM))
        outs[5].append(v8[:, :dec].reshape(bs, dec, ATT_HEADS, ATT_HEAD_DIM))
        outs[6].append(s_new)
        outs[7].append(nconv.reshape(CONV_WIDTH - 1, bs, DN_QKV).transpose(1, 0, 2))

    stacked = [jnp.stack(o) for o in outs]
    return (xp, xs_bm, stacked[0], stacked[1], stacked[2], stacked[3],
            stacked[4], stacked[5], stacked[6], stacked[7])
```

```python
import functools
import math

import jax
import jax.numpy as jnp
from jax import lax
from jax.experimental import pallas as pl
from jax.experimental.pallas import tpu as pltpu

F32 = jnp.float32
BF16 = jnp.bfloat16
HIGHEST = lax.Precision.HIGHEST

ATT_HEADS = 8
ATT_HEAD_DIM = 64
ATT_WIDTH = ATT_HEADS * ATT_HEAD_DIM
DN_HEADS = 4
DN_HEAD_DIM = 128
DN_WIDTH = DN_HEADS * DN_HEAD_DIM
DN_QKV = 3 * DN_WIDTH
MAIN_COLS = 4 * ATT_WIDTH + DN_QKV + DN_WIDTH
MOBA_BLOCK = 256
MOBA_TOPK = 3
DN_CHUNK = 64
CONV_WIDTH = 4
PAGE_SIZE = 128
ROPE_THETA = 10000.0
LN_EPS = 1e-5
RMS_EPS = 1e-6
L2_EPS = 1e-6
ATT_SCALE = ATT_HEAD_DIM ** -0.5
QK_SCALE = ATT_SCALE * math.log2(math.e)
NEG = -1e30
LANES = 128
SUBLANES = 8
TOK_ROWS = 2 * SUBLANES
VT_ROWS = ATT_HEAD_DIM + 16
MAX_MOBA_BLOCKS = LANES - ATT_HEAD_DIM
PAGES_PER_STEP = 8
VMEM_LIMIT = 56 * 1024 * 1024

NN_DIMS = (((1,), (0,)), ((), ()))
NT_DIMS = (((1,), (1,)), ((), ()))
TN_DIMS = (((0,), (0,)), ((), ()))


def _sigmoid(x):
    return 1.0 / (1.0 + jnp.exp(-x))


def _silu(x):
    return x * _sigmoid(x)


def _softplus(x):
    return jnp.maximum(x, 0.0) + jnp.log1p(jnp.exp(-jnp.abs(x)))


def _split_bf16(a):
    hi = a.astype(BF16)
    return hi, (a - hi.astype(F32)).astype(BF16)


def _mm3(a, b, dims=NN_DIMS):
    ah, al = _split_bf16(a)
    bh, bl = _split_bf16(b)

    def dot(x, y):
        return lax.dot_general(x, y, dims, preferred_element_type=F32)

    return (dot(ah, bl) + dot(al, bh)) + dot(ah, bh)


def _cparams(sem):
    return pltpu.CompilerParams(dimension_semantics=sem, vmem_limit_bytes=VMEM_LIMIT)


def _mod_kernel(c_ref, w_ref, b_ref, o_ref):
    s = _silu(c_ref[...])
    o_ref[0] = jnp.dot(s.astype(BF16), w_ref[0].astype(BF16), preferred_element_type=F32) + b_ref[0]


def _mod_call(c_all, w_ada, b_ada):
    depth, d, d3 = w_ada.shape
    rows = c_all.shape[0]
    nj = d3 // d
    return pl.pallas_call(
        _mod_kernel,
        grid=(depth, nj),
        in_specs=[
            pl.BlockSpec((rows, d), lambda l, j: (0, 0)),
            pl.BlockSpec((1, d, d), lambda l, j: (l, 0, j)),
            pl.BlockSpec((1, 1, d), lambda l, j: (l, 0, j)),
        ],
        out_specs=pl.BlockSpec((1, rows, d), lambda l, j: (l, 0, j)),
        out_shape=jax.ShapeDtypeStruct((depth, rows, d3), F32),
        compiler_params=_cparams(("arbitrary", "arbitrary")),
        name="mod",
    )(c_all, w_ada, b_ada.reshape(depth, 1, d3))


def _rope(p, cos, sin_signed):
    lane = lax.broadcasted_iota(jnp.int32, (1, LANES), 1)
    first_half = (lane % ATT_HEAD_DIM) < (ATT_HEAD_DIM // 2)
    outs = []
    for j in range(ATT_WIDTH // LANES):
        xj = p[:, j * LANES:(j + 1) * LANES]
        partner = jnp.where(first_half, pltpu.roll(xj, LANES - 32, 1), pltpu.roll(xj, 32, 1))
        outs.append(xj * cos + partner * sin_signed)
    return jnp.concatenate(outs, axis=-1)


def _l2norm_heads(c, scale):
    outs = []
    for h in range(DN_HEADS):
        ch = c[:, h * DN_HEAD_DIM:(h + 1) * DN_HEAD_DIM]
        ss = jnp.sum(ch * ch, axis=-1, keepdims=True)
        outs.append(ch * (lax.rsqrt(ss + L2_EPS) * scale))
    return jnp.concatenate(outs, axis=-1)


def _beta_decay(h, ws_ref, alog_ref, dtb_ref):
    raw = jnp.dot(h, ws_ref[...], precision=HIGHEST, preferred_element_type=F32)
    lane = lax.broadcasted_iota(jnp.int32, (1, LANES), 1)
    beta = _sigmoid(raw)
    g = -jnp.exp(alog_ref[...]) * _softplus(raw + dtb_ref[...])
    return jnp.where(lane < DN_HEADS, beta, jnp.where(lane < 2 * DN_HEADS, g, 0.0))


def _proj(hb, w_ref, g):
    return jnp.dot(hb, w_ref[:, g * ATT_WIDTH:(g + 1) * ATT_WIDTH], preferred_element_type=F32)


def _head_slabs(p, fill):
    lane = lax.broadcasted_iota(jnp.int32, (1, LANES), 1)
    low = lane < ATT_HEAD_DIM
    outs = []
    for j in range(ATT_WIDTH // LANES):
        xj = p[:, j * LANES:(j + 1) * LANES]
        outs.append(jnp.where(low, xj, fill))
        outs.append(jnp.where(low, pltpu.roll(xj, ATT_HEAD_DIM, 1), fill))
    return jnp.concatenate(outs, axis=-1)


def _in_prompt_kernel(x_ref, mod_ref, w_ref, ws_ref, cw_ref, alog_ref, dtb_ref, cos_ref, sin_ref,
                      qa_ref, k_ref, v_ref, ka_ref, vt_ref, km_ref, sza_ref,
                      qd_ref, kd_ref, vd_ref, szd_ref, gb_ref, nc_ref, xbuf, *, tm):
    t = pl.program_id(1)
    d = x_ref.shape[-1]
    hist = SUBLANES

    @pl.when(t == 0)
    def _():
        xbuf[0:hist, :] = jnp.zeros((hist, DN_QKV), F32)

    x = x_ref[0]
    shift = mod_ref[0][:, 0:d]
    scale = mod_ref[0][:, d:2 * d]
    h = x * (1.0 + scale) + shift
    hb = h.astype(BF16)
    cos = cos_ref[...]
    sin = sin_ref[...]

    lane = lax.broadcasted_iota(jnp.int32, (1, LANES), 1)
    qa_ref[0] = _head_slabs(_rope(_proj(hb, w_ref, 0), cos, sin) * QK_SCALE, 0.0)
    kr = _rope(_proj(hb, w_ref, 1), cos, sin)
    k_ref[0] = kr
    block_onehot = (lane == ATT_HEAD_DIM + t).astype(F32)
    ka_ref[0] = _head_slabs(kr, block_onehot).astype(BF16)
    km_ref[0, 0] = jnp.sum(kr, axis=0, keepdims=True) * (1.0 / tm)
    v = _proj(hb, w_ref, 2)
    v_ref[0] = v
    vt = v.T
    for hd in range(ATT_HEADS):
        vt_ref[0, 0, hd, 0:ATT_HEAD_DIM, :] = vt[hd * ATT_HEAD_DIM:(hd + 1) * ATT_HEAD_DIM, :].astype(BF16)
        vt_ref[0, 0, hd, ATT_HEAD_DIM:VT_ROWS, :] = jnp.ones((VT_ROWS - ATT_HEAD_DIM, tm), BF16)
    sza_ref[0] = _silu(_proj(hb, w_ref, 3))
    szd_ref[0] = _silu(_proj(hb, w_ref, 7))
    gb_ref[0] = _beta_decay(h, ws_ref, alog_ref, dtb_ref)

    outs = (qd_ref, kd_ref, vd_ref)
    for gi in range(3):
        cols = slice(gi * DN_WIDTH, (gi + 1) * DN_WIDTH)
        xbuf[hist:hist + tm, cols] = _proj(hb, w_ref, 4 + gi)
        conv = jnp.zeros((tm, DN_WIDTH), F32)
        for i in range(CONV_WIDTH):
            off = hist - (CONV_WIDTH - 1) + i
            conv = conv + xbuf[off:off + tm, cols] * cw_ref[i:i + 1, cols]
        c = _silu(conv)
        if gi == 0:
            c = _l2norm_heads(c, DN_HEAD_DIM ** -0.5)
        elif gi == 1:
            c = _l2norm_heads(c, 1.0)
        outs[gi][0] = c

    tail = xbuf[tm:tm + hist, :]
    nc_ref[0] = tail
    xbuf[0:hist, :] = tail


def _in_prompt_call(x, mod, w_main, w_small, conv_w, alog, dtb, cos, sin, tm):
    n, t, d = x.shape
    nt = t // tm
    row = lambda n_, t_: (n_, t_, 0)
    full = lambda n_, t_: (0, 0)
    wide = jax.ShapeDtypeStruct((n, t, ATT_WIDTH), F32)
    blk = pl.BlockSpec((1, tm, ATT_WIDTH), row)
    slab_w = ATT_HEADS * LANES
    slab_blk = pl.BlockSpec((1, tm, slab_w), row)
    return pl.pallas_call(
        functools.partial(_in_prompt_kernel, tm=tm),
        grid=(n, nt),
        in_specs=[
            pl.BlockSpec((1, tm, d), row),
            pl.BlockSpec((1, 1, 3 * d), lambda n_, t_: (n_, 0, 0)),
            pl.BlockSpec((d, MAIN_COLS), full),
            pl.BlockSpec((d, LANES), full),
            pl.BlockSpec((CONV_WIDTH, DN_QKV), full),
            pl.BlockSpec((1, LANES), full),
            pl.BlockSpec((1, LANES), full),
            pl.BlockSpec((tm, LANES), lambda n_, t_: (t_, 0)),
            pl.BlockSpec((tm, LANES), lambda n_, t_: (t_, 0)),
        ],
        out_specs=[
            slab_blk, blk, blk, slab_blk,
            pl.BlockSpec((1, 1, ATT_HEADS, VT_ROWS, tm), lambda n_, t_: (n_, t_, 0, 0, 0)),
            pl.BlockSpec((1, 1, 1, ATT_WIDTH), lambda n_, t_: (n_, t_, 0, 0)),
            blk, blk, blk, blk, blk,
            pl.BlockSpec((1, tm, LANES), row),
            pl.BlockSpec((1, SUBLANES, DN_QKV), lambda n_, t_: (n_, 0, 0)),
        ],
        out_shape=[
            jax.ShapeDtypeStruct((n, t, slab_w), F32), wide, wide,
            jax.ShapeDtypeStruct((n, t, slab_w), BF16),
            jax.ShapeDtypeStruct((n, nt, ATT_HEADS, VT_ROWS, tm), BF16),
            jax.ShapeDtypeStruct((n, nt, 1, ATT_WIDTH), F32),
            wide, wide, wide, wide, wide,
            jax.ShapeDtypeStruct((n, t, LANES), F32),
            jax.ShapeDtypeStruct((n, SUBLANES, DN_QKV), F32),
        ],
        scratch_shapes=[pltpu.VMEM((tm + SUBLANES, DN_QKV), F32)],
        compiler_params=_cparams(("arbitrary", "arbitrary")),
        name="in_prompt",
    )(x, mod, w_main, w_small, conv_w, alog, dtb, cos, sin)


def _in_sample_kernel(x_ref, shift_ref, scale_ref, w_ref, ws_ref, cw_ref, alog_ref, dtb_ref,
                      cos_ref, sin_ref, hist_ref,
                      q_ref, k_ref, v_ref, sza_ref, qd_ref, kd_ref, vd_ref, szd_ref, gb_ref, nc_ref,
                      xall, *, batch, steps):
    rows = batch * steps
    nh = (CONV_WIDTH - 1) * batch
    h = x_ref[...] * (1.0 + scale_ref[...]) + shift_ref[...]
    hb = h.astype(BF16)
    cos = cos_ref[...]
    sin = sin_ref[...]
    q_ref[...] = _rope(_proj(hb, w_ref, 0), cos, sin) * QK_SCALE
    k_ref[...] = _rope(_proj(hb, w_ref, 1), cos, sin)
    v_ref[...] = _proj(hb, w_ref, 2)
    sza_ref[...] = _silu(_proj(hb, w_ref, 3))
    szd_ref[...] = _silu(_proj(hb, w_ref, 7))
    gb_ref[...] = _beta_decay(h, ws_ref, alog_ref, dtb_ref)

    xall[0:nh, :] = hist_ref[...]
    outs = (qd_ref, kd_ref, vd_ref)
    for gi in range(3):
        cols = slice(gi * DN_WIDTH, (gi + 1) * DN_WIDTH)
        xall[nh:nh + rows, cols] = _proj(hb, w_ref, 4 + gi)
        conv = jnp.zeros((rows, DN_WIDTH), F32)
        for i in range(CONV_WIDTH):
            conv = conv + xall[i * batch:i * batch + rows, cols] * cw_ref[i:i + 1, cols]
        c = _silu(conv)
        if gi == 0:
            c = _l2norm_heads(c, DN_HEAD_DIM ** -0.5)
        elif gi == 1:
            c = _l2norm_heads(c, 1.0)
        outs[gi][...] = c
    nc_ref[...] = xall[rows:rows + nh, :]


def _in_sample_call(x_tm, shift, scale, w_main, w_small, conv_w, alog, dtb, cos, sin, hist_tm, batch, steps):
    rows = batch * steps
    nh = (CONV_WIDTH - 1) * batch
    wide = jax.ShapeDtypeStruct((rows, ATT_WIDTH), F32)
    return pl.pallas_call(
        functools.partial(_in_sample_kernel, batch=batch, steps=steps),
        out_shape=[wide] * 8 + [jax.ShapeDtypeStruct((rows, LANES), F32),
                                jax.ShapeDtypeStruct((nh, DN_QKV), F32)],
        scratch_shapes=[pltpu.VMEM((rows + nh, DN_QKV), F32)],
        compiler_params=pltpu.CompilerParams(vmem_limit_bytes=VMEM_LIMIT),
        name="in_sample",
    )(x_tm, shift, scale, w_main, w_small, conv_w, alog, dtb, cos, sin, hist_tm)


def _topk_bias(gate, valid, blkf, axis=-1):
    gate = jnp.where(valid, gate, -jnp.inf)
    bias = jnp.full(gate.shape, NEG, F32)
    for _ in range(MOBA_TOPK):
        mx = jnp.max(gate, axis=axis, keepdims=True)
        cand = (gate == mx) & (mx > -jnp.inf)
        idx = jnp.min(jnp.where(cand, blkf, 1e9), axis=axis, keepdims=True)
        pick = blkf == idx
        bias = jnp.where(pick, 0.0, bias)
        gate = jnp.where(pick, -jnp.inf, gate)
    return bias


def _moba_prompt_kernel(q_ref, k_ref, vt_ref, km_ref, o_ref, s_ref, m_ref, acc_ref):
    i = pl.program_id(2)
    bs = MOBA_BLOCK
    lane = lax.broadcasted_iota(jnp.int32, (1, LANES), 1)
    low = lane < ATT_HEAD_DIM
    brow = lax.broadcasted_iota(jnp.int32, (MAX_MOBA_BLOCKS, bs), 0)
    browf = brow.astype(F32)
    krow = lax.broadcasted_iota(jnp.int32, (bs, bs), 0)
    qcol = lax.broadcasted_iota(jnp.int32, (bs, bs), 1)

    def pv(j, hh, p):
        return jnp.dot(vt_ref[0, j, hh], p.astype(BF16), preferred_element_type=F32)

    qas = []
    for hh in range(2):
        qs = q_ref[0][:, hh * LANES:(hh + 1) * LANES]
        gate_t = lax.dot_general(km_ref[0, hh], qs, NT_DIMS, precision=HIGHEST,
                                 preferred_element_type=F32)
        bias = _topk_bias(gate_t[ATT_HEAD_DIM:, :], brow < i, browf, axis=0)
        bias = jnp.where(brow == i, 0.0, bias)
        bias_t = jnp.concatenate([jnp.full((ATT_HEAD_DIM, bs), NEG, F32), bias], axis=0)
        qas.append(jnp.where(low, qs, bias_t.T).astype(BF16))

    def scores(j):
        off = pl.multiple_of(j * bs, bs)
        return [lax.dot_general(k_ref[0, pl.ds(off, bs), hh * LANES:(hh + 1) * LANES], qas[hh],
                                NT_DIMS, preferred_element_type=F32) for hh in range(2)]

    for hh, s in enumerate(scores(i)):
        s = jnp.where(krow <= qcol, s, NEG)
        m = jnp.max(s, axis=0, keepdims=True)
        m_ref[hh] = m
        acc_ref[hh] = pv(i, hh, jnp.exp2(s - m))
    for hh, s in enumerate(scores(jnp.maximum(i - 1, 0))):
        s_ref[0, hh] = s

    def body(t, carry):
        j = i - 1 - t
        slot = t % 2
        cur = [s_ref[slot, hh] for hh in range(2)]
        nxt = scores(jnp.maximum(j - 1, 0))
        for hh in range(2):
            m = m_ref[hh]
            m_new = jnp.maximum(m, jnp.max(cur[hh], axis=0, keepdims=True))
            acc_ref[hh] = jnp.exp2(m - m_new) * acc_ref[hh] + pv(j, hh, jnp.exp2(cur[hh] - m_new))
            m_ref[hh] = m_new
        for hh in range(2):
            s_ref[1 - slot, hh] = nxt[hh]
        return carry

    lax.fori_loop(0, i, body, 0)
    d = ATT_HEAD_DIM
    a0, a1 = acc_ref[0], acc_ref[1]
    o = jnp.concatenate([a0[0:d] / a0[d:d + 1], a1[0:d] / a1[d:d + 1]], axis=0)
    o_ref[0] = o.T


def _moba_prompt_call(q_aug, k_aug, vt, km_pad):
    n, t, _ = q_aug.shape
    nb = t // MOBA_BLOCK
    npair = ATT_WIDTH // LANES
    pair_w = 2 * LANES
    return pl.pallas_call(
        _moba_prompt_kernel,
        grid=(n, npair, nb),
        in_specs=[
            pl.BlockSpec((1, MOBA_BLOCK, pair_w), lambda n_, p_, i_: (n_, i_, p_)),
            pl.BlockSpec((1, t, pair_w), lambda n_, p_, i_: (n_, 0, p_)),
            pl.BlockSpec((1, nb, 2, VT_ROWS, MOBA_BLOCK), lambda n_, p_, i_: (n_, 0, p_, 0, 0)),
            pl.BlockSpec((1, 2, LANES, LANES), lambda n_, p_, i_: (n_, p_, 0, 0)),
        ],
        out_specs=pl.BlockSpec((1, MOBA_BLOCK, LANES), lambda n_, p_, i_: (n_, i_, p_)),
        out_shape=jax.ShapeDtypeStruct((n, t, ATT_WIDTH), F32),
        scratch_shapes=[
            pltpu.VMEM((2, 2, MOBA_BLOCK, MOBA_BLOCK), F32),
            pltpu.VMEM((2, 1, MOBA_BLOCK), F32),
            pltpu.VMEM((2, VT_ROWS, MOBA_BLOCK), F32),
        ],
        compiler_params=_cparams(("arbitrary", "arbitrary", "arbitrary")),
        name="moba_prompt",
    )(q_aug, k_aug, vt, km_pad)


def _moba_sample_kernel(pt_ref, q_ref, kn_ref, vn_ref, *rest, npages, steps):
    pps = PAGES_PER_STEP
    kp = rest[:pps]
    vp = rest[pps:2 * pps]
    o_ref = rest[2 * pps]
    lg_ref, km_ref, acc_ref, lsum_ref = rest[2 * pps + 1:]
    s = pl.program_id(1)
    nsteps = npages // pps
    nblk = npages * PAGE_SIZE // MOBA_BLOCK
    ppb = MOBA_BLOCK // PAGE_SIZE
    hs = range(ATT_HEADS)
    half = TOK_ROWS // 2

    def head_of(page_ref, h):
        return page_ref[pl.ds(h, PAGE_SIZE, stride=ATT_HEADS), :]

    @pl.when(s < nsteps)
    def _():
        qb = [q_ref[0, h].astype(BF16) for h in hs]
        ksum = [None] * ATT_HEADS
        for i in range(pps):
            page = s * pps + i
            for h in hs:
                kh = head_of(kp[i], h)
                lg = lax.dot_general(qb[h], kh.astype(BF16), NT_DIMS, preferred_element_type=F32)
                lg_ref[page, h] = lg[0:half]
                part = jnp.sum(kh, axis=0, keepdims=True)
                ksum[h] = part if i % ppb == 0 else ksum[h] + part
                if i % ppb == ppb - 1:
                    km_ref[h, pl.ds(s * (pps // ppb) + i // ppb, 1), :] = ksum[h] * (1.0 / MOBA_BLOCK)

    @pl.when(s == nsteps - 1)
    def _():
        rows = ATT_HEADS * half
        blk = lax.broadcasted_iota(jnp.int32, (rows, nblk), 1)
        trow = lax.broadcasted_iota(jnp.int32, (rows, TOK_ROWS), 0) % half
        tcol = lax.broadcasted_iota(jnp.int32, (rows, TOK_ROWS), 1)
        qf = [q_ref[0, h] for h in hs]
        gate = jnp.concatenate(
            [lax.dot_general(qf[h][0:half], km_ref[h], NT_DIMS, precision=HIGHEST,
                             preferred_element_type=F32) for h in hs], axis=0)
        bias = _topk_bias(gate, blk >= 0, blk.astype(F32))
        lo = jnp.concatenate(
            [lax.dot_general(qf[h].astype(BF16), kn_ref[0, h].astype(BF16), NT_DIMS,
                             preferred_element_type=F32)[0:half] for h in hs], axis=0)
        lo = jnp.where((tcol <= trow) & (tcol < steps), lo, NEG)
        bcols = [[bias[h * half:(h + 1) * half, b:b + 1] for b in range(nblk)] for h in hs]
        mts = []
        for h in hs:
            mt = jnp.full((half, PAGE_SIZE), NEG, F32)
            for b in range(nblk):
                for pg in range(ppb):
                    mt = jnp.maximum(mt, lg_ref[b * ppb + pg, h] + bcols[h][b])
            mts.append(mt)
        m = jnp.maximum(jnp.max(jnp.concatenate(mts, axis=0), axis=-1, keepdims=True),
                        jnp.max(lo, axis=-1, keepdims=True))
        po = jnp.exp2(lo - m)
        lts = []
        for h in hs:
            mh = m[h * half:(h + 1) * half]
            lt = jnp.zeros((half, PAGE_SIZE), F32)
            for b in range(nblk):
                for pg in range(ppb):
                    pb = jnp.exp2(lg_ref[b * ppb + pg, h] + bcols[h][b] - mh)
                    lg_ref[b * ppb + pg, h] = pb
                    lt = lt + pb
            lts.append(lt)
        lsum = (jnp.sum(jnp.concatenate(lts, axis=0), axis=-1, keepdims=True)
                + jnp.sum(po, axis=-1, keepdims=True))
        zeros = jnp.zeros((half, TOK_ROWS), F32)
        for h in hs:
            lsum_ref[h] = lsum[h * half:(h + 1) * half]
            po16 = jnp.concatenate([po[h * half:(h + 1) * half], zeros], axis=0).astype(BF16)
            acc_ref[h] = jnp.dot(po16, vn_ref[0, h].astype(BF16), preferred_element_type=F32)[0:half]

    @pl.when(s >= nsteps)
    def _():
        acc = [acc_ref[h] for h in hs]
        zeros = jnp.zeros((half, PAGE_SIZE), F32)
        for i in range(pps):
            page = (s - nsteps) * pps + i
            for h in hs:
                p16 = jnp.concatenate([lg_ref[page, h], zeros], axis=0).astype(BF16)
                vh = head_of(vp[i], h).astype(BF16)
                acc[h] = acc[h] + jnp.dot(p16, vh, preferred_element_type=F32)[0:half]
        for h in hs:
            acc_ref[h] = acc[h]

    @pl.when(s == 2 * nsteps - 1)
    def _():
        for h in hs:
            o_ref[0, h] = acc_ref[h] / lsum_ref[h]


def _moba_sample_call(page_flat, qh, knh, vnh, cache_k4, cache_v4, layer, npages, steps):
    batch = qh.shape[0]
    pps = PAGES_PER_STEP
    nsteps = npages // pps
    half = TOK_ROWS // 2
    page_rows = PAGE_SIZE * ATT_HEADS
    tok = pl.BlockSpec((1, ATT_HEADS, TOK_ROWS, ATT_HEAD_DIM), lambda n_, s_, pt: (n_, 0, 0, 0))

    def kspec(i):
        return pl.BlockSpec(
            (None, None, page_rows, ATT_HEAD_DIM),
            lambda n_, s_, pt: (layer, pt[n_ * npages + jnp.minimum(s_, nsteps - 1) * pps + i], 0, 0))

    def vspec(i):
        return pl.BlockSpec(
            (None, None, page_rows, ATT_HEAD_DIM),
            lambda n_, s_, pt: (layer, pt[n_ * npages + jnp.maximum(s_ - nsteps, 0) * pps + i], 0, 0))

    grid_spec = pltpu.PrefetchScalarGridSpec(
        num_scalar_prefetch=1,
        grid=(batch, 2 * nsteps),
        in_specs=[tok, tok, tok] + [kspec(i) for i in range(pps)] + [vspec(i) for i in range(pps)],
        out_specs=pl.BlockSpec((1, ATT_HEADS, half, ATT_HEAD_DIM), lambda n_, s_, pt: (n_, 0, 0, 0)),
        scratch_shapes=[
            pltpu.VMEM((npages, ATT_HEADS, half, PAGE_SIZE), F32),
            pltpu.VMEM((ATT_HEADS, npages * PAGE_SIZE // MOBA_BLOCK, ATT_HEAD_DIM), F32),
            pltpu.VMEM((ATT_HEADS, half, ATT_HEAD_DIM), F32),
            pltpu.VMEM((ATT_HEADS, half, 1), F32),
        ],
    )
    return pl.pallas_call(
        functools.partial(_moba_sample_kernel, npages=npages, steps=steps),
        grid_spec=grid_spec,
        out_shape=jax.ShapeDtypeStruct((batch, ATT_HEADS, half, ATT_HEAD_DIM), F32),
        compiler_params=_cparams(("arbitrary", "arbitrary")),
        name="moba_sample",
    )(page_flat, qh, knh, vnh, *([cache_k4] * pps), *([cache_v4] * pps))


def _gdn_kernel(q_ref, k_ref, v_ref, gb_ref, s0_ref, o_ref, sout_ref, s_scr, *, chunk):
    c = pl.program_id(1)
    nc = pl.num_programs(1)

    @pl.when(c == 0)
    def _():
        s_scr[...] = s0_ref[0]

    gb = gb_ref[0]
    ri = lax.broadcasted_iota(jnp.int32, (chunk, chunk), 0)
    ci = lax.broadcasted_iota(jnp.int32, (chunk, chunk), 1)
    tril = ri >= ci
    stril = ri > ci
    eye = (ri == ci).astype(F32)
    lane = lax.broadcasted_iota(jnp.int32, (1, LANES), 1)
    gcum_all = jnp.dot(tril.astype(F32), gb, precision=HIGHEST, preferred_element_type=F32)
    gcum_rows = lax.dot_general(gcum_all, eye, TN_DIMS, precision=HIGHEST, preferred_element_type=F32)
    nsq = int(round(math.log2(chunk))) - 1

    def mm1(a, b, dims=NN_DIMS):
        return lax.dot_general(a.astype(BF16), b.astype(BF16), dims, preferred_element_type=F32)

    hs = range(DN_HEADS)
    cols = [slice(h * DN_HEAD_DIM, (h + 1) * DN_HEAD_DIM) for h in hs]
    q = [q_ref[0][:, cols[h]] for h in hs]
    k = [k_ref[0][:, cols[h]] for h in hs]
    v = [v_ref[0][:, cols[h]] for h in hs]
    beta = [gb[:, h:h + 1] for h in hs]
    gc = [gcum_all[:, DN_HEADS + h:DN_HEADS + h + 1] for h in hs]
    kbeta = [k[h] * beta[h] for h in hs]
    kk = [_mm3(kbeta[h], k[h], NT_DIMS) for h in hs]
    qk = [mm1(q[h], k[h], NT_DIMS) for h in hs]
    decay = []
    for h in hs:
        diff = gc[h] - gcum_rows[DN_HEADS + h:DN_HEADS + h + 1, :]
        decay.append(jnp.where(tril, jnp.exp(jnp.where(tril, diff, 0.0)), 0.0))
    eg = [jnp.exp(gc[h]) for h in hs]
    pw = [-jnp.where(stril, kk[h] * decay[h], 0.0) for h in hs]
    y = [jnp.concatenate([v[h] * beta[h], kbeta[h] * eg[h]], axis=-1) for h in hs]
    y = [y[h] + _mm3(pw[h], y[h]) for h in hs]
    for _ in range(nsq):
        pw = [_mm3(pw[h], pw[h]) for h in hs]
        y = [y[h] + _mm3(pw[h], y[h]) for h in hs]
    s_old = [s_scr[h] for h in hs]
    v_new = [y[h][:, 0:DN_HEAD_DIM] - _mm3(y[h][:, DN_HEAD_DIM:2 * DN_HEAD_DIM], s_old[h]) for h in hs]
    g_last = [gc[h][chunk - 1:chunk, :] for h in hs]
    kv = [_mm3(k[h] * jnp.exp(g_last[h] - gc[h]), v_new[h], TN_DIMS) for h in hs]
    for h in hs:
        s_scr[h] = s_old[h] * jnp.exp(g_last[h]) + kv[h]
    for h in hs:
        attn = jnp.where(tril, qk[h] * decay[h], 0.0)
        o_ref[0, :, cols[h]] = mm1(q[h] * eg[h], s_old[h]) + mm1(attn, v_new[h])

    @pl.when(c == nc - 1)
    def _():
        sout_ref[0] = s_scr[...]


def _gdn_call(qd, kd, vd, gb, s0, s0_index, chunk):
    n, t, _ = qd.shape
    nc = t // chunk
    row = lambda n_, c_: (n_, c_, 0)
    blk = pl.BlockSpec((1, chunk, DN_WIDTH), row)
    sshape = (DN_HEADS, DN_HEAD_DIM, DN_HEAD_DIM)
    s0_block = (None,) * (s0.ndim - 4) + (1,) + sshape
    return pl.pallas_call(
        functools.partial(_gdn_kernel, chunk=chunk),
        grid=(n, nc),
        in_specs=[blk, blk, blk,
                  pl.BlockSpec((1, chunk, LANES), row),
                  pl.BlockSpec(s0_block, lambda n_, c_: s0_index(n_))],
        out_specs=[blk, pl.BlockSpec((1,) + sshape, lambda n_, c_: (n_, 0, 0, 0))],
        out_shape=[jax.ShapeDtypeStruct((n, t, DN_WIDTH), F32),
                   jax.ShapeDtypeStruct((n,) + sshape, F32)],
        scratch_shapes=[pltpu.VMEM(sshape, F32)],
        compiler_params=_cparams(("arbitrary", "arbitrary")),
        name="gdn",
    )(qd, kd, vd, gb, s0)


def _out_kernel(oa_ref, sza_ref, od_ref, szd_ref, x_ref, gate_ref, wo_ref, nw_ref, lg_ref, lb_ref,
                o_ref, *, alpha):
    ya = (oa_ref[0] * sza_ref[0]).astype(BF16)
    od = od_ref[0]
    parts = []
    for h in range(DN_HEADS):
        oh = od[:, h * DN_HEAD_DIM:(h + 1) * DN_HEAD_DIM]
        ms = jnp.mean(oh * oh, axis=-1, keepdims=True)
        parts.append(oh * lax.rsqrt(ms + RMS_EPS) * nw_ref[...])
    yd = (jnp.concatenate(parts, axis=-1) * szd_ref[0]).astype(BF16)
    y = (jnp.dot(ya, wo_ref[0:ATT_WIDTH, :], preferred_element_type=F32)
         + jnp.dot(yd, wo_ref[ATT_WIDTH:ATT_WIDTH + DN_WIDTH, :], preferred_element_type=F32))
    r = alpha * x_ref[0] + (1.0 + gate_ref[0]) * y
    mu = jnp.mean(r, axis=-1, keepdims=True)
    rc = r - mu
    var = jnp.mean(rc * rc, axis=-1, keepdims=True)
    o_ref[0] = rc * lax.rsqrt(var + LN_EPS) * lg_ref[...] + lb_ref[...]


def _out_call(oa, sza, od, szd, x, gate, w_out_bf, nw, lg, lb, tm, alpha):
    n, t, d = x.shape
    nt = t // tm
    row = lambda n_, t_: (n_, t_, 0)
    full = lambda n_, t_: (0, 0)
    blk = pl.BlockSpec((1, tm, ATT_WIDTH), row)
    if gate.shape[1] == 1:
        gate_spec = pl.BlockSpec((1, 1, d), lambda n_, t_: (n_, 0, 0))
    else:
        gate_spec = pl.BlockSpec((1, tm, d), row)
    return pl.pallas_call(
        functools.partial(_out_kernel, alpha=alpha),
        grid=(n, nt),
        in_specs=[blk, blk, blk, blk,
                  pl.BlockSpec((1, tm, d), row),
                  gate_spec,
                  pl.BlockSpec((ATT_WIDTH + DN_WIDTH, d), full),
                  pl.BlockSpec((1, DN_HEAD_DIM), full),
                  pl.BlockSpec((1, d), full),
                  pl.BlockSpec((1, d), full)],
        out_specs=pl.BlockSpec((1, tm, d), row),
        out_shape=jax.ShapeDtypeStruct((n, t, d), F32),
        compiler_params=_cparams(("arbitrary", "arbitrary")),
        name="out_proj",
    )(oa, sza, od, szd, x, gate, w_out_bf, nw, lg, lb)


def _rope_tables(pos):
    half = ATT_HEAD_DIM // 2
    inv_freq = ROPE_THETA ** (-jnp.arange(half, dtype=F32) / half)
    ang = pos.astype(F32)[:, None] * inv_freq[None, :]
    cos = jnp.cos(ang)
    sin = jnp.sin(ang)
    reps = LANES // ATT_HEAD_DIM
    return jnp.tile(cos, (1, 2 * reps)), jnp.tile(jnp.concatenate([-sin, sin], axis=1), (1, reps))


def _lane_row(vec, offset):
    return jnp.zeros((1, LANES), F32).at[0, offset:offset + vec.shape[0]].set(vec.astype(F32))


def kernel(x_prompt, x_sample, c_prompt, c_sample, cache_k, cache_v, page_table, state_ssm, state_conv,
           w_ada, b_ada, w_in, conv_w, a_log, dt_bias, dn_norm_w, w_out, ln_g, ln_b):
    depth, d, _ = w_ada.shape
    bp, seq, _ = x_prompt.shape
    bs, dec, _ = x_sample.shape
    npages = page_table.shape[1]
    past = npages * PAGE_SIZE
    n_pool = cache_k.shape[1]
    alpha = float((2 * depth) ** 0.25)
    tm = MOBA_BLOCK
    nb = seq // tm
    assert seq % tm == 0 and seq % DN_CHUNK == 0 and past % MOBA_BLOCK == 0 and nb <= MAX_MOBA_BLOCKS
    assert dec <= SUBLANES and npages % PAGES_PER_STEP == 0

    nc_rows = bp + bs
    c_rows = -(-nc_rows // SUBLANES) * SUBLANES
    c_all = jnp.zeros((c_rows, d), F32).at[:bp].set(c_prompt).at[bp:nc_rows].set(c_sample)
    mod_all = _mod_call(c_all, w_ada, b_ada)

    cos_p, sin_p = _rope_tables(jnp.arange(seq, dtype=jnp.int32))
    pos_s = past + jnp.repeat(jnp.arange(dec, dtype=jnp.int32), bs)
    cos_s, sin_s = _rope_tables(pos_s)

    cache_k4 = cache_k.reshape(depth, n_pool, PAGE_SIZE * ATT_HEADS, ATT_HEAD_DIM)
    cache_v4 = cache_v.reshape(depth, n_pool, PAGE_SIZE * ATT_HEADS, ATT_HEAD_DIM)
    page_flat = page_table.reshape(-1).astype(jnp.int32)
    zero_state = jnp.zeros((bp, DN_HEADS, DN_HEAD_DIM, DN_HEAD_DIM), F32)

    def to_bm8(a):
        w = a.shape[-1]
        a = a.reshape(dec, bs, w).transpose(1, 0, 2)
        return jnp.pad(a, ((0, 0), (0, SUBLANES - dec), (0, 0)))

    def to_heads(a):
        a = a.reshape(dec, bs, ATT_HEADS, ATT_HEAD_DIM).transpose(1, 2, 0, 3)
        return jnp.pad(a, ((0, 0), (0, 0), (0, TOK_ROWS - dec), (0, 0)))

    xp = x_prompt
    xs_tm = x_sample.transpose(1, 0, 2).reshape(dec * bs, d)
    outs = [[] for _ in range(8)]
    for l in range(depth):
        w_main = w_in[l, :, :MAIN_COLS].astype(BF16)
        w_small = jnp.zeros((d, LANES), F32).at[:, :2 * DN_HEADS].set(w_in[l, :, MAIN_COLS:])
        alog = _lane_row(a_log[l], DN_HEADS)
        dtb = _lane_row(dt_bias[l], DN_HEADS)
        w_out_bf = w_out[l].astype(BF16)
        nw = dn_norm_w[l].reshape(1, DN_HEAD_DIM)
        lg = ln_g[l].reshape(1, d)
        lb = ln_b[l].reshape(1, d)
        mod_p = mod_all[l, :bp].reshape(bp, 1, 3 * d)
        mod_s = mod_all[l, bp:nc_rows]

        (q_aug, k, v, k_aug, vt, kmean, sza, qd, kd, vd, szd, gb, nconv) = _in_prompt_call(
            xp, mod_p, w_main, w_small, conv_w[l], alog, dtb, cos_p, sin_p, tm)
        km_heads = kmean.reshape(bp, nb, ATT_HEADS, ATT_HEAD_DIM).transpose(0, 2, 1, 3)
        km_pad = jnp.pad(km_heads, ((0, 0), (0, 0), (ATT_HEAD_DIM, LANES - ATT_HEAD_DIM - nb),
                                    (0, LANES - ATT_HEAD_DIM)))
        oa = _moba_prompt_call(q_aug, k_aug, vt, km_pad)
        od, s_new = _gdn_call(qd, kd, vd, gb, zero_state, lambda n_: (n_, 0, 0, 0), DN_CHUNK)
        xp = _out_call(oa, sza, od, szd, xp, mod_p[:, :, 2 * d:], w_out_bf, nw, lg, lb, tm, alpha)
        outs[0].append(k.reshape(bp, seq, ATT_HEADS, ATT_HEAD_DIM))
        outs[1].append(v.reshape(bp, seq, ATT_HEADS, ATT_HEAD_DIM))
        outs[2].append(s_new)
        outs[3].append(nconv[:, SUBLANES - (CONV_WIDTH - 1):, :])

        mod_rows = jnp.tile(mod_s, (dec, 1))
        hist_tm = state_conv[l].transpose(1, 0, 2).reshape((CONV_WIDTH - 1) * bs, DN_QKV)
        (q, k, v, sza, qd, kd, vd, szd, gb, nconv) = _in_sample_call(
            xs_tm, mod_rows[:, :d], mod_rows[:, d:2 * d], w_main, w_small, conv_w[l], alog, dtb,
            cos_s, sin_s, hist_tm, bs, dec)
        k8, v8 = to_bm8(k), to_bm8(v)
        oh = _moba_sample_call(page_flat, to_heads(q), to_heads(k), to_heads(v), cache_k4, cache_v4,
                               l, npages, dec)
        oa8 = oh.transpose(0, 2, 1, 3).reshape(bs, SUBLANES, ATT_WIDTH)
        od8, s_new = _gdn_call(to_bm8(qd), to_bm8(kd), to_bm8(vd), to_bm8(gb), state_ssm,
                               lambda n_, l=l: (l, n_, 0, 0, 0), SUBLANES)
        rows8 = bs * SUBLANES
        flat = lambda a: a.reshape(1, rows8, a.shape[-1])
        gate_rows = jnp.repeat(mod_s[:, 2 * d:], SUBLANES, axis=0).reshape(1, rows8, d)
        xs8 = _out_call(flat(oa8), flat(to_bm8(sza)), flat(od8), flat(to_bm8(szd)), flat(to_bm8(xs_tm)),
                        gate_rows, w_out_bf, nw, lg, lb, rows8, alpha)
        xs_bm = xs8.reshape(bs, SUBLANES, d)[:, :dec]
        xs_tm = xs_bm.transpose(1, 0, 2).reshape(dec * bs, d)
        outs[4].append(k8[:, :dec].reshape(bs, dec, ATT_HEADS, ATT_HEAD_DIM))
        outs[5].append(v8[:, :dec].reshape(bs, dec, ATT_HEADS, ATT_HEAD_DIM))
        outs[6].append(s_new)
        outs[7].append(nconv.reshape(CONV_WIDTH - 1, bs, DN_QKV).transpose(1, 0, 2))

    stacked = [jnp.stack(o) for o in outs]
    return (xp, xs_bm, stacked[0], stacked[1], stacked[2], stacked[3],
            stacked[4], stacked[5], stacked[6], stacked[7])
```

```python
import functools
import math

import jax
import jax.numpy as jnp
from jax import lax
from jax.experimental import pallas as pl
from jax.experimental.pallas import tpu as pltpu

F32 = jnp.float32
BF16 = jnp.bfloat16
HIGHEST = lax.Precision.HIGHEST

ATT_HEADS = 8
ATT_HEAD_DIM = 64
ATT_WIDTH = ATT_HEADS * ATT_HEAD_DIM
DN_HEADS = 4
DN_HEAD_DIM = 128
DN_WIDTH = DN_HEADS * DN_HEAD_DIM
DN_QKV = 3 * DN_WIDTH
MAIN_COLS = 4 * ATT_WIDTH + DN_QKV + DN_WIDTH
MOBA_BLOCK = 256
MOBA_TOPK = 3
DN_CHUNK = 64
CONV_WIDTH = 4
PAGE_SIZE = 128
ROPE_THETA = 10000.0
LN_EPS = 1e-5
RMS_EPS = 1e-6
L2_EPS = 1e-6
ATT_SCALE = ATT_HEAD_DIM ** -0.5
QK_SCALE = ATT_SCALE * math.log2(math.e)
NEG = -1e30
LANES = 128
SUBLANES = 8
TOK_ROWS = 2 * SUBLANES
VT_ROWS = ATT_HEAD_DIM + 16
MAX_MOBA_BLOCKS = LANES - ATT_HEAD_DIM
PAGES_PER_STEP = 16
VMEM_LIMIT = 56 * 1024 * 1024

NN_DIMS = (((1,), (0,)), ((), ()))
NT_DIMS = (((1,), (1,)), ((), ()))
TN_DIMS = (((0,), (0,)), ((), ()))


def _sigmoid(x):
    return 1.0 / (1.0 + jnp.exp(-x))


def _silu(x):
    return x * _sigmoid(x)


def _softplus(x):
    return jnp.maximum(x, 0.0) + jnp.log1p(jnp.exp(-jnp.abs(x)))


def _split_bf16(a):
    hi = a.astype(BF16)
    return hi, (a - hi.astype(F32)).astype(BF16)


def _mm3(a, b, dims=NN_DIMS):
    ah, al = _split_bf16(a)
    bh, bl = _split_bf16(b)

    def dot(x, y):
        return lax.dot_general(x, y, dims, preferred_element_type=F32)

    return (dot(ah, bl) + dot(al, bh)) + dot(ah, bh)


def _cparams(sem):
    return pltpu.CompilerParams(dimension_semantics=sem, vmem_limit_bytes=VMEM_LIMIT)


def _mod_kernel(c_ref, w_ref, b_ref, o_ref):
    s = _silu(c_ref[...])
    o_ref[0] = jnp.dot(s.astype(BF16), w_ref[0].astype(BF16), preferred_element_type=F32) + b_ref[0]


def _mod_call(c_all, w_ada, b_ada):
    depth, d, d3 = w_ada.shape
    rows = c_all.shape[0]
    nj = d3 // d
    return pl.pallas_call(
        _mod_kernel,
        grid=(depth, nj),
        in_specs=[
            pl.BlockSpec((rows, d), lambda l, j: (0, 0)),
            pl.BlockSpec((1, d, d), lambda l, j: (l, 0, j)),
            pl.BlockSpec((1, 1, d), lambda l, j: (l, 0, j)),
        ],
        out_specs=pl.BlockSpec((1, rows, d), lambda l, j: (l, 0, j)),
        out_shape=jax.ShapeDtypeStruct((depth, rows, d3), F32),
        compiler_params=_cparams(("arbitrary", "arbitrary")),
        name="mod",
    )(c_all, w_ada, b_ada.reshape(depth, 1, d3))


def _rope(p, cos, sin_signed):
    lane = lax.broadcasted_iota(jnp.int32, (1, LANES), 1)
    first_half = (lane % ATT_HEAD_DIM) < (ATT_HEAD_DIM // 2)
    outs = []
    for j in range(ATT_WIDTH // LANES):
        xj = p[:, j * LANES:(j + 1) * LANES]
        partner = jnp.where(first_half, pltpu.roll(xj, LANES - 32, 1), pltpu.roll(xj, 32, 1))
        outs.append(xj * cos + partner * sin_signed)
    return jnp.concatenate(outs, axis=-1)


def _l2norm_heads(c, scale):
    outs = []
    for h in range(DN_HEADS):
        ch = c[:, h * DN_HEAD_DIM:(h + 1) * DN_HEAD_DIM]
        ss = jnp.sum(ch * ch, axis=-1, keepdims=True)
        outs.append(ch * (lax.rsqrt(ss + L2_EPS) * scale))
    return jnp.concatenate(outs, axis=-1)


def _beta_decay(h, ws_ref, alog_ref, dtb_ref):
    raw = jnp.dot(h, ws_ref[...], precision=HIGHEST, preferred_element_type=F32)
    lane = lax.broadcasted_iota(jnp.int32, (1, LANES), 1)
    beta = _sigmoid(raw)
    g = -jnp.exp(alog_ref[...]) * _softplus(raw + dtb_ref[...])
    return jnp.where(lane < DN_HEADS, beta, jnp.where(lane < 2 * DN_HEADS, g, 0.0))


def _proj(hb, w_ref, g):
    return jnp.dot(hb, w_ref[:, g * ATT_WIDTH:(g + 1) * ATT_WIDTH], preferred_element_type=F32)


def _head_slabs(p, fill):
    lane = lax.broadcasted_iota(jnp.int32, (1, LANES), 1)
    low = lane < ATT_HEAD_DIM
    outs = []
    for j in range(ATT_WIDTH // LANES):
        xj = p[:, j * LANES:(j + 1) * LANES]
        outs.append(jnp.where(low, xj, fill))
        outs.append(jnp.where(low, pltpu.roll(xj, ATT_HEAD_DIM, 1), fill))
    return jnp.concatenate(outs, axis=-1)


def _in_prompt_kernel(x_ref, mod_ref, w_ref, ws_ref, cw_ref, alog_ref, dtb_ref, cos_ref, sin_ref,
                      qa_ref, k_ref, v_ref, ka_ref, vt_ref, km_ref, sza_ref,
                      qd_ref, kd_ref, vd_ref, szd_ref, gb_ref, nc_ref, xbuf, *, tm):
    t = pl.program_id(1)
    d = x_ref.shape[-1]
    hist = SUBLANES

    @pl.when(t == 0)
    def _():
        xbuf[0:hist, :] = jnp.zeros((hist, DN_QKV), F32)

    x = x_ref[0]
    shift = mod_ref[0][:, 0:d]
    scale = mod_ref[0][:, d:2 * d]
    h = x * (1.0 + scale) + shift
    hb = h.astype(BF16)
    cos = cos_ref[...]
    sin = sin_ref[...]

    lane = lax.broadcasted_iota(jnp.int32, (1, LANES), 1)
    qa_ref[0] = _head_slabs(_rope(_proj(hb, w_ref, 0), cos, sin) * QK_SCALE, 0.0)
    kr = _rope(_proj(hb, w_ref, 1), cos, sin)
    k_ref[0] = kr
    block_onehot = (lane == ATT_HEAD_DIM + t).astype(F32)
    ka_ref[0] = _head_slabs(kr, block_onehot).astype(BF16)
    km_ref[0, 0] = jnp.sum(kr, axis=0, keepdims=True) * (1.0 / tm)
    v = _proj(hb, w_ref, 2)
    v_ref[0] = v
    vt = v.T
    for hd in range(ATT_HEADS):
        vt_ref[0, 0, hd, 0:ATT_HEAD_DIM, :] = vt[hd * ATT_HEAD_DIM:(hd + 1) * ATT_HEAD_DIM, :].astype(BF16)
        vt_ref[0, 0, hd, ATT_HEAD_DIM:VT_ROWS, :] = jnp.ones((VT_ROWS - ATT_HEAD_DIM, tm), BF16)
    sza_ref[0] = _silu(_proj(hb, w_ref, 3))
    szd_ref[0] = _silu(_proj(hb, w_ref, 7))
    gb_ref[0] = _beta_decay(h, ws_ref, alog_ref, dtb_ref)

    outs = (qd_ref, kd_ref, vd_ref)
    for gi in range(3):
        cols = slice(gi * DN_WIDTH, (gi + 1) * DN_WIDTH)
        xbuf[hist:hist + tm, cols] = _proj(hb, w_ref, 4 + gi)
        conv = jnp.zeros((tm, DN_WIDTH), F32)
        for i in range(CONV_WIDTH):
            off = hist - (CONV_WIDTH - 1) + i
            conv = conv + xbuf[off:off + tm, cols] * cw_ref[i:i + 1, cols]
        c = _silu(conv)
        if gi == 0:
            c = _l2norm_heads(c, DN_HEAD_DIM ** -0.5)
        elif gi == 1:
            c = _l2norm_heads(c, 1.0)
        outs[gi][0] = c

    tail = xbuf[tm:tm + hist, :]
    nc_ref[0] = tail
    xbuf[0:hist, :] = tail


def _in_prompt_call(x, mod, w_main, w_small, conv_w, alog, dtb, cos, sin, tm):
    n, t, d = x.shape
    nt = t // tm
    row = lambda n_, t_: (n_, t_, 0)
    full = lambda n_, t_: (0, 0)
    wide = jax.ShapeDtypeStruct((n, t, ATT_WIDTH), F32)
    blk = pl.BlockSpec((1, tm, ATT_WIDTH), row)
    slab_w = ATT_HEADS * LANES
    slab_blk = pl.BlockSpec((1, tm, slab_w), row)
    return pl.pallas_call(
        functools.partial(_in_prompt_kernel, tm=tm),
        grid=(n, nt),
        in_specs=[
            pl.BlockSpec((1, tm, d), row),
            pl.BlockSpec((1, 1, 3 * d), lambda n_, t_: (n_, 0, 0)),
            pl.BlockSpec((d, MAIN_COLS), full),
            pl.BlockSpec((d, LANES), full),
            pl.BlockSpec((CONV_WIDTH, DN_QKV), full),
            pl.BlockSpec((1, LANES), full),
            pl.BlockSpec((1, LANES), full),
            pl.BlockSpec((tm, LANES), lambda n_, t_: (t_, 0)),
            pl.BlockSpec((tm, LANES), lambda n_, t_: (t_, 0)),
        ],
        out_specs=[
            slab_blk, blk, blk, slab_blk,
            pl.BlockSpec((1, 1, ATT_HEADS, VT_ROWS, tm), lambda n_, t_: (n_, t_, 0, 0, 0)),
            pl.BlockSpec((1, 1, 1, ATT_WIDTH), lambda n_, t_: (n_, t_, 0, 0)),
            blk, blk, blk, blk, blk,
            pl.BlockSpec((1, tm, LANES), row),
            pl.BlockSpec((1, SUBLANES, DN_QKV), lambda n_, t_: (n_, 0, 0)),
        ],
        out_shape=[
            jax.ShapeDtypeStruct((n, t, slab_w), F32), wide, wide,
            jax.ShapeDtypeStruct((n, t, slab_w), BF16),
            jax.ShapeDtypeStruct((n, nt, ATT_HEADS, VT_ROWS, tm), BF16),
            jax.ShapeDtypeStruct((n, nt, 1, ATT_WIDTH), F32),
            wide, wide, wide, wide, wide,
            jax.ShapeDtypeStruct((n, t, LANES), F32),
            jax.ShapeDtypeStruct((n, SUBLANES, DN_QKV), F32),
        ],
        scratch_shapes=[pltpu.VMEM((tm + SUBLANES, DN_QKV), F32)],
        compiler_params=_cparams(("arbitrary", "arbitrary")),
        name="in_prompt",
    )(x, mod, w_main, w_small, conv_w, alog, dtb, cos, sin)


def _in_sample_kernel(x_ref, shift_ref, scale_ref, w_ref, ws_ref, cw_ref, alog_ref, dtb_ref,
                      cos_ref, sin_ref, hist_ref,
                      q_ref, k_ref, v_ref, sza_ref, qd_ref, kd_ref, vd_ref, szd_ref, gb_ref, nc_ref,
                      xall, *, batch, steps):
    rows = batch * steps
    nh = (CONV_WIDTH - 1) * batch
    h = x_ref[...] * (1.0 + scale_ref[...]) + shift_ref[...]
    hb = h.astype(BF16)
    cos = cos_ref[...]
    sin = sin_ref[...]
    q_ref[...] = _rope(_proj(hb, w_ref, 0), cos, sin) * QK_SCALE
    k_ref[...] = _rope(_proj(hb, w_ref, 1), cos, sin)
    v_ref[...] = _proj(hb, w_ref, 2)
    sza_ref[...] = _silu(_proj(hb, w_ref, 3))
    szd_ref[...] = _silu(_proj(hb, w_ref, 7))
    gb_ref[...] = _beta_decay(h, ws_ref, alog_ref, dtb_ref)

    xall[0:nh, :] = hist_ref[...]
    outs = (qd_ref, kd_ref, vd_ref)
    for gi in range(3):
        cols = slice(gi * DN_WIDTH, (gi + 1) * DN_WIDTH)
        xall[nh:nh + rows, cols] = _proj(hb, w_ref, 4 + gi)
        conv = jnp.zeros((rows, DN_WIDTH), F32)
        for i in range(CONV_WIDTH):
            conv = conv + xall[i * batch:i * batch + rows, cols] * cw_ref[i:i + 1, cols]
        c = _silu(conv)
        if gi == 0:
            c = _l2norm_heads(c, DN_HEAD_DIM ** -0.5)
        elif gi == 1:
            c = _l2norm_heads(c, 1.0)
        outs[gi][...] = c
    nc_ref[...] = xall[rows:rows + nh, :]


def _in_sample_call(x_tm, shift, scale, w_main, w_small, conv_w, alog, dtb, cos, sin, hist_tm, batch, steps):
    rows = batch * steps
    nh = (CONV_WIDTH - 1) * batch
    wide = jax.ShapeDtypeStruct((rows, ATT_WIDTH), F32)
    return pl.pallas_call(
        functools.partial(_in_sample_kernel, batch=batch, steps=steps),
        out_shape=[wide] * 8 + [jax.ShapeDtypeStruct((rows, LANES), F32),
                                jax.ShapeDtypeStruct((nh, DN_QKV), F32)],
        scratch_shapes=[pltpu.VMEM((rows + nh, DN_QKV), F32)],
        compiler_params=pltpu.CompilerParams(vmem_limit_bytes=VMEM_LIMIT),
        name="in_sample",
    )(x_tm, shift, scale, w_main, w_small, conv_w, alog, dtb, cos, sin, hist_tm)


def _topk_bias(gate, valid, blkf, axis=-1):
    gate = jnp.where(valid, gate, -jnp.inf)
    bias = jnp.full(gate.shape, NEG, F32)
    for _ in range(MOBA_TOPK):
        mx = jnp.max(gate, axis=axis, keepdims=True)
        cand = (gate == mx) & (mx > -jnp.inf)
        idx = jnp.min(jnp.where(cand, blkf, 1e9), axis=axis, keepdims=True)
        pick = blkf == idx
        bias = jnp.where(pick, 0.0, bias)
        gate = jnp.where(pick, -jnp.inf, gate)
    return bias


def _moba_prompt_kernel(q_ref, k_ref, vt_ref, km_ref, o_ref, s_ref, m_ref, acc_ref):
    i = pl.program_id(2)
    bs = MOBA_BLOCK
    lane = lax.broadcasted_iota(jnp.int32, (1, LANES), 1)
    low = lane < ATT_HEAD_DIM
    brow = lax.broadcasted_iota(jnp.int32, (MAX_MOBA_BLOCKS, bs), 0)
    browf = brow.astype(F32)
    krow = lax.broadcasted_iota(jnp.int32, (bs, bs), 0)
    qcol = lax.broadcasted_iota(jnp.int32, (bs, bs), 1)

    def pv(j, hh, p):
        return jnp.dot(vt_ref[0, j, hh], p.astype(BF16), preferred_element_type=F32)

    qas = []
    for hh in range(2):
        qs = q_ref[0][:, hh * LANES:(hh + 1) * LANES]
        gate_t = lax.dot_general(km_ref[0, hh], qs, NT_DIMS, precision=HIGHEST,
                                 preferred_element_type=F32)
        bias = _topk_bias(gate_t[ATT_HEAD_DIM:, :], brow < i, browf, axis=0)
        bias = jnp.where(brow == i, 0.0, bias)
        bias_t = jnp.concatenate([jnp.full((ATT_HEAD_DIM, bs), NEG, F32), bias], axis=0)
        qas.append(jnp.where(low, qs, bias_t.T).astype(BF16))

    def scores(j):
        off = pl.multiple_of(j * bs, bs)
        return [lax.dot_general(k_ref[0, pl.ds(off, bs), hh * LANES:(hh + 1) * LANES], qas[hh],
                                NT_DIMS, preferred_element_type=F32) for hh in range(2)]

    for hh, s in enumerate(scores(i)):
        s = jnp.where(krow <= qcol, s, NEG)
        m = jnp.max(s, axis=0, keepdims=True)
        m_ref[hh] = m
        acc_ref[hh] = pv(i, hh, jnp.exp2(s - m))
    for hh, s in enumerate(scores(jnp.maximum(i - 1, 0))):
        s_ref[0, hh] = s

    def body(t, carry):
        j = i - 1 - t
        slot = t % 2
        cur = [s_ref[slot, hh] for hh in range(2)]
        nxt = scores(jnp.maximum(j - 1, 0))
        for hh in range(2):
            m = m_ref[hh]
            m_new = jnp.maximum(m, jnp.max(cur[hh], axis=0, keepdims=True))
            acc_ref[hh] = jnp.exp2(m - m_new) * acc_ref[hh] + pv(j, hh, jnp.exp2(cur[hh] - m_new))
            m_ref[hh] = m_new
        for hh in range(2):
            s_ref[1 - slot, hh] = nxt[hh]
        return carry

    lax.fori_loop(0, i, body, 0)
    d = ATT_HEAD_DIM
    a0, a1 = acc_ref[0], acc_ref[1]
    o = jnp.concatenate([a0[0:d] / a0[d:d + 1], a1[0:d] / a1[d:d + 1]], axis=0)
    o_ref[0] = o.T


def _moba_prompt_call(q_aug, k_aug, vt, km_pad):
    n, t, _ = q_aug.shape
    nb = t // MOBA_BLOCK
    npair = ATT_WIDTH // LANES
    pair_w = 2 * LANES
    return pl.pallas_call(
        _moba_prompt_kernel,
        grid=(n, npair, nb),
        in_specs=[
            pl.BlockSpec((1, MOBA_BLOCK, pair_w), lambda n_, p_, i_: (n_, i_, p_)),
            pl.BlockSpec((1, t, pair_w), lambda n_, p_, i_: (n_, 0, p_)),
            pl.BlockSpec((1, nb, 2, VT_ROWS, MOBA_BLOCK), lambda n_, p_, i_: (n_, 0, p_, 0, 0)),
            pl.BlockSpec((1, 2, LANES, LANES), lambda n_, p_, i_: (n_, p_, 0, 0)),
        ],
        out_specs=pl.BlockSpec((1, MOBA_BLOCK, LANES), lambda n_, p_, i_: (n_, i_, p_)),
        out_shape=jax.ShapeDtypeStruct((n, t, ATT_WIDTH), F32),
        scratch_shapes=[
            pltpu.VMEM((2, 2, MOBA_BLOCK, MOBA_BLOCK), F32),
            pltpu.VMEM((2, 1, MOBA_BLOCK), F32),
            pltpu.VMEM((2, VT_ROWS, MOBA_BLOCK), F32),
        ],
        compiler_params=_cparams(("arbitrary", "arbitrary", "arbitrary")),
        name="moba_prompt",
    )(q_aug, k_aug, vt, km_pad)


def _moba_sample_kernel(pt_ref, q_ref, kn_ref, vn_ref, *rest, npages, steps):
    pps = PAGES_PER_STEP
    kp = rest[:pps]
    vp = rest[pps:2 * pps]
    o_ref = rest[2 * pps]
    lg_ref, km_ref, acc_ref, lsum_ref = rest[2 * pps + 1:]
    s = pl.program_id(1)
    nsteps = npages // pps
    nblk = npages * PAGE_SIZE // MOBA_BLOCK
    ppb = MOBA_BLOCK // PAGE_SIZE
    hs = range(ATT_HEADS)
    half = TOK_ROWS // 2

    lane = lax.broadcasted_iota(jnp.int32, (1, LANES), 1)

    @pl.when(s == 0)
    def _():
        km_ref[...] = jnp.zeros(km_ref.shape, F32)

    @pl.when(s < nsteps)
    def _():
        qb = [q_ref[0, h].astype(BF16) for h in hs]
        ksum = [None] * ATT_HEADS
        for i in range(pps):
            page = s * pps + i
            for h in hs:
                kt = kp[i][h]
                lg = jnp.dot(qb[h], kt.astype(BF16), preferred_element_type=F32)
                lg_ref[page, h] = lg[0:half]
                ksum[h] = kt if i % ppb == 0 else ksum[h] + kt
                if i % ppb == ppb - 1:
                    col = jnp.sum(ksum[h], axis=-1, keepdims=True) * (1.0 / MOBA_BLOCK)
                    km_ref[h] = jnp.where(lane == s * (pps // ppb) + i // ppb, col, km_ref[h])

    @pl.when(s == nsteps - 1)
    def _():
        rows = ATT_HEADS * half
        blk = lax.broadcasted_iota(jnp.int32, (rows, LANES), 1)
        trow = lax.broadcasted_iota(jnp.int32, (rows, TOK_ROWS), 0) % half
        tcol = lax.broadcasted_iota(jnp.int32, (rows, TOK_ROWS), 1)
        qf = [q_ref[0, h] for h in hs]
        gate = jnp.concatenate(
            [jnp.dot(qf[h][0:half], km_ref[h], precision=HIGHEST, preferred_element_type=F32)
             for h in hs], axis=0)
        bias = _topk_bias(gate, blk < nblk, blk.astype(F32))
        lo = jnp.concatenate(
            [lax.dot_general(qf[h].astype(BF16), kn_ref[0, h].astype(BF16), NT_DIMS,
                             preferred_element_type=F32)[0:half] for h in hs], axis=0)
        lo = jnp.where((tcol <= trow) & (tcol < steps), lo, NEG)
        bcols = [[bias[h * half:(h + 1) * half, b:b + 1] for b in range(nblk)] for h in hs]
        mts = []
        for h in hs:
            mt = jnp.full((half, PAGE_SIZE), NEG, F32)
            for b in range(nblk):
                for pg in range(ppb):
                    mt = jnp.maximum(mt, lg_ref[b * ppb + pg, h] + bcols[h][b])
            mts.append(mt)
        m = jnp.maximum(jnp.max(jnp.concatenate(mts, axis=0), axis=-1, keepdims=True),
                        jnp.max(lo, axis=-1, keepdims=True))
        po = jnp.exp2(lo - m)
        lts = []
        for h in hs:
            mh = m[h * half:(h + 1) * half]
            lt = jnp.zeros((half, PAGE_SIZE), F32)
            for b in range(nblk):
                for pg in range(ppb):
                    pb = jnp.exp2(lg_ref[b * ppb + pg, h] + bcols[h][b] - mh)
                    lg_ref[b * ppb + pg, h] = pb
                    lt = lt + pb
            lts.append(lt)
        lsum = (jnp.sum(jnp.concatenate(lts, axis=0), axis=-1, keepdims=True)
                + jnp.sum(po, axis=-1, keepdims=True))
        zeros = jnp.zeros((half, TOK_ROWS), F32)
        for h in hs:
            lsum_ref[h] = lsum[h * half:(h + 1) * half]
            po16 = jnp.concatenate([po[h * half:(h + 1) * half], zeros], axis=0).astype(BF16)
            acc_ref[h] = jnp.dot(po16, vn_ref[0, h].astype(BF16), preferred_element_type=F32)[0:half]

    @pl.when(s >= nsteps)
    def _():
        acc = [acc_ref[h] for h in hs]
        zeros = jnp.zeros((half, PAGE_SIZE), F32)
        for i in range(pps):
            page = (s - nsteps) * pps + i
            for h in hs:
                p16 = jnp.concatenate([lg_ref[page, h], zeros], axis=0).astype(BF16)
                vt = vp[i][h].astype(BF16)
                acc[h] = acc[h] + lax.dot_general(p16, vt, NT_DIMS, preferred_element_type=F32)[0:half]
        for h in hs:
            acc_ref[h] = acc[h]

    @pl.when(s == 2 * nsteps - 1)
    def _():
        for h in hs:
            o_ref[0, h] = acc_ref[h] / lsum_ref[h]


def _moba_sample_call(page_flat, qh, knh, vnh, cache_k4, cache_v4, layer, npages, steps):
    batch = qh.shape[0]
    pps = PAGES_PER_STEP
    nsteps = npages // pps
    half = TOK_ROWS // 2
    page_block = (None, None, ATT_HEADS, ATT_HEAD_DIM, PAGE_SIZE)
    tok = pl.BlockSpec((1, ATT_HEADS, TOK_ROWS, ATT_HEAD_DIM), lambda n_, s_, pt: (n_, 0, 0, 0))

    def kspec(i):
        return pl.BlockSpec(
            page_block,
            lambda n_, s_, pt: (layer, pt[n_ * npages + jnp.minimum(s_, nsteps - 1) * pps + i], 0, 0, 0))

    def vspec(i):
        return pl.BlockSpec(
            page_block,
            lambda n_, s_, pt: (layer, pt[n_ * npages + jnp.maximum(s_ - nsteps, 0) * pps + i], 0, 0, 0))

    grid_spec = pltpu.PrefetchScalarGridSpec(
        num_scalar_prefetch=1,
        grid=(batch, 2 * nsteps),
        in_specs=[tok, tok, tok] + [kspec(i) for i in range(pps)] + [vspec(i) for i in range(pps)],
        out_specs=pl.BlockSpec((1, ATT_HEADS, half, ATT_HEAD_DIM), lambda n_, s_, pt: (n_, 0, 0, 0)),
        scratch_shapes=[
            pltpu.VMEM((npages, ATT_HEADS, half, PAGE_SIZE), F32),
            pltpu.VMEM((ATT_HEADS, ATT_HEAD_DIM, LANES), F32),
            pltpu.VMEM((ATT_HEADS, half, ATT_HEAD_DIM), F32),
            pltpu.VMEM((ATT_HEADS, half, 1), F32),
        ],
    )
    return pl.pallas_call(
        functools.partial(_moba_sample_kernel, npages=npages, steps=steps),
        grid_spec=grid_spec,
        out_shape=jax.ShapeDtypeStruct((batch, ATT_HEADS, half, ATT_HEAD_DIM), F32),
        compiler_params=_cparams(("arbitrary", "arbitrary")),
        name="moba_sample",
    )(page_flat, qh, knh, vnh, *([cache_k4] * pps), *([cache_v4] * pps))


def _gdn_kernel(q_ref, k_ref, v_ref, gb_ref, s0_ref, o_ref, sout_ref, s_scr, *, chunk):
    c = pl.program_id(1)
    nc = pl.num_programs(1)

    @pl.when(c == 0)
    def _():
        s_scr[...] = s0_ref[0]

    gb = gb_ref[0]
    ri = lax.broadcasted_iota(jnp.int32, (chunk, chunk), 0)
    ci = lax.broadcasted_iota(jnp.int32, (chunk, chunk), 1)
    tril = ri >= ci
    stril = ri > ci
    eye = (ri == ci).astype(F32)
    lane = lax.broadcasted_iota(jnp.int32, (1, LANES), 1)
    gcum_all = jnp.dot(tril.astype(F32), gb, precision=HIGHEST, preferred_element_type=F32)
    gcum_rows = lax.dot_general(gcum_all, eye, TN_DIMS, precision=HIGHEST, preferred_element_type=F32)
    nsq = int(round(math.log2(chunk))) - 1

    def mm1(a, b, dims=NN_DIMS):
        return lax.dot_general(a.astype(BF16), b.astype(BF16), dims, preferred_element_type=F32)

    hs = range(DN_HEADS)
    cols = [slice(h * DN_HEAD_DIM, (h + 1) * DN_HEAD_DIM) for h in hs]
    q = [q_ref[0][:, cols[h]] for h in hs]
    k = [k_ref[0][:, cols[h]] for h in hs]
    v = [v_ref[0][:, cols[h]] for h in hs]
    beta = [gb[:, h:h + 1] for h in hs]
    gc = [gcum_all[:, DN_HEADS + h:DN_HEADS + h + 1] for h in hs]
    kbeta = [k[h] * beta[h] for h in hs]
    kk = [_mm3(kbeta[h], k[h], NT_DIMS) for h in hs]
    qk = [mm1(q[h], k[h], NT_DIMS) for h in hs]
    decay = []
    for h in hs:
        diff = gc[h] - gcum_rows[DN_HEADS + h:DN_HEADS + h + 1, :]
        decay.append(jnp.where(tril, jnp.exp(jnp.where(tril, diff, 0.0)), 0.0))
    eg = [jnp.exp(gc[h]) for h in hs]
    pw = [-jnp.where(stril, kk[h] * decay[h], 0.0) for h in hs]
    y = [jnp.concatenate([v[h] * beta[h], kbeta[h] * eg[h]], axis=-1) for h in hs]
    y = [y[h] + _mm3(pw[h], y[h]) for h in hs]
    for _ in range(nsq):
        pw = [_mm3(pw[h], pw[h]) for h in hs]
        y = [y[h] + _mm3(pw[h], y[h]) for h in hs]
    s_old = [s_scr[h] for h in hs]
    v_new = [y[h][:, 0:DN_HEAD_DIM] - _mm3(y[h][:, DN_HEAD_DIM:2 * DN_HEAD_DIM], s_old[h]) for h in hs]
    g_last = [gc[h][chunk - 1:chunk, :] for h in hs]
    kv = [_mm3(k[h] * jnp.exp(g_last[h] - gc[h]), v_new[h], TN_DIMS) for h in hs]
    for h in hs:
        s_scr[h] = s_old[h] * jnp.exp(g_last[h]) + kv[h]
    for h in hs:
        attn = jnp.where(tril, qk[h] * decay[h], 0.0)
        o_ref[0, :, cols[h]] = mm1(q[h] * eg[h], s_old[h]) + mm1(attn, v_new[h])

    @pl.when(c == nc - 1)
    def _():
        sout_ref[0] = s_scr[...]


def _gdn_call(qd, kd, vd, gb, s0, s0_index, chunk):
    n, t, _ = qd.shape
    nc = t // chunk
    row = lambda n_, c_: (n_, c_, 0)
    blk = pl.BlockSpec((1, chunk, DN_WIDTH), row)
    sshape = (DN_HEADS, DN_HEAD_DIM, DN_HEAD_DIM)
    s0_block = (None,) * (s0.ndim - 4) + (1,) + sshape
    return pl.pallas_call(
        functools.partial(_gdn_kernel, chunk=chunk),
        grid=(n, nc),
        in_specs=[blk, blk, blk,
                  pl.BlockSpec((1, chunk, LANES), row),
                  pl.BlockSpec(s0_block, lambda n_, c_: s0_index(n_))],
        out_specs=[blk, pl.BlockSpec((1,) + sshape, lambda n_, c_: (n_, 0, 0, 0))],
        out_shape=[jax.ShapeDtypeStruct((n, t, DN_WIDTH), F32),
                   jax.ShapeDtypeStruct((n,) + sshape, F32)],
        scratch_shapes=[pltpu.VMEM(sshape, F32)],
        compiler_params=_cparams(("arbitrary", "arbitrary")),
        name="gdn",
    )(qd, kd, vd, gb, s0)


def _out_kernel(oa_ref, sza_ref, od_ref, szd_ref, x_ref, gate_ref, wo_ref, nw_ref, lg_ref, lb_ref,
                o_ref, *, alpha):
    ya = (oa_ref[0] * sza_ref[0]).astype(BF16)
    od = od_ref[0]
    parts = []
    for h in range(DN_HEADS):
        oh = od[:, h * DN_HEAD_DIM:(h + 1) * DN_HEAD_DIM]
        ms = jnp.mean(oh * oh, axis=-1, keepdims=True)
        parts.append(oh * lax.rsqrt(ms + RMS_EPS) * nw_ref[...])
    yd = (jnp.concatenate(parts, axis=-1) * szd_ref[0]).astype(BF16)
    y = (jnp.dot(ya, wo_ref[0:ATT_WIDTH, :], preferred_element_type=F32)
         + jnp.dot(yd, wo_ref[ATT_WIDTH:ATT_WIDTH + DN_WIDTH, :], preferred_element_type=F32))
    r = alpha * x_ref[0] + (1.0 + gate_ref[0]) * y
    mu = jnp.mean(r, axis=-1, keepdims=True)
    rc = r - mu
    var = jnp.mean(rc * rc, axis=-1, keepdims=True)
    o_ref[0] = rc * lax.rsqrt(var + LN_EPS) * lg_ref[...] + lb_ref[...]


def _out_call(oa, sza, od, szd, x, gate, w_out_bf, nw, lg, lb, tm, alpha):
    n, t, d = x.shape
    nt = t // tm
    row = lambda n_, t_: (n_, t_, 0)
    full = lambda n_, t_: (0, 0)
    blk = pl.BlockSpec((1, tm, ATT_WIDTH), row)
    if gate.shape[1] == 1:
        gate_spec = pl.BlockSpec((1, 1, d), lambda n_, t_: (n_, 0, 0))
    else:
        gate_spec = pl.BlockSpec((1, tm, d), row)
    return pl.pallas_call(
        functools.partial(_out_kernel, alpha=alpha),
        grid=(n, nt),
        in_specs=[blk, blk, blk, blk,
                  pl.BlockSpec((1, tm, d), row),
                  gate_spec,
                  pl.BlockSpec((ATT_WIDTH + DN_WIDTH, d), full),
                  pl.BlockSpec((1, DN_HEAD_DIM), full),
                  pl.BlockSpec((1, d), full),
                  pl.BlockSpec((1, d), full)],
        out_specs=pl.BlockSpec((1, tm, d), row),
        out_shape=jax.ShapeDtypeStruct((n, t, d), F32),
        compiler_params=_cparams(("arbitrary", "arbitrary")),
        name="out_proj",
    )(oa, sza, od, szd, x, gate, w_out_bf, nw, lg, lb)


def _rope_tables(pos):
    half = ATT_HEAD_DIM // 2
    inv_freq = ROPE_THETA ** (-jnp.arange(half, dtype=F32) / half)
    ang = pos.astype(F32)[:, None] * inv_freq[None, :]
    cos = jnp.cos(ang)
    sin = jnp.sin(ang)
    reps = LANES // ATT_HEAD_DIM
    return jnp.tile(cos, (1, 2 * reps)), jnp.tile(jnp.concatenate([-sin, sin], axis=1), (1, reps))


def _lane_row(vec, offset):
    return jnp.zeros((1, LANES), F32).at[0, offset:offset + vec.shape[0]].set(vec.astype(F32))


def kernel(x_prompt, x_sample, c_prompt, c_sample, cache_k, cache_v, page_table, state_ssm, state_conv,
           w_ada, b_ada, w_in, conv_w, a_log, dt_bias, dn_norm_w, w_out, ln_g, ln_b):
    depth, d, _ = w_ada.shape
    bp, seq, _ = x_prompt.shape
    bs, dec, _ = x_sample.shape
    npages = page_table.shape[1]
    past = npages * PAGE_SIZE
    n_pool = cache_k.shape[1]
    alpha = float((2 * depth) ** 0.25)
    tm = MOBA_BLOCK
    nb = seq // tm
    assert seq % tm == 0 and seq % DN_CHUNK == 0 and past % MOBA_BLOCK == 0 and nb <= MAX_MOBA_BLOCKS
    assert dec <= SUBLANES and npages % PAGES_PER_STEP == 0

    nc_rows = bp + bs
    c_rows = -(-nc_rows // SUBLANES) * SUBLANES
    c_all = jnp.zeros((c_rows, d), F32).at[:bp].set(c_prompt).at[bp:nc_rows].set(c_sample)
    mod_all = _mod_call(c_all, w_ada, b_ada)

    cos_p, sin_p = _rope_tables(jnp.arange(seq, dtype=jnp.int32))
    pos_s = past + jnp.repeat(jnp.arange(dec, dtype=jnp.int32), bs)
    cos_s, sin_s = _rope_tables(pos_s)

    cache_kt = cache_k.transpose(0, 1, 3, 4, 2)
    cache_vt = cache_v.transpose(0, 1, 3, 4, 2)
    page_flat = page_table.reshape(-1).astype(jnp.int32)
    zero_state = jnp.zeros((bp, DN_HEADS, DN_HEAD_DIM, DN_HEAD_DIM), F32)

    def to_bm8(a):
        w = a.shape[-1]
        a = a.reshape(dec, bs, w).transpose(1, 0, 2)
        return jnp.pad(a, ((0, 0), (0, SUBLANES - dec), (0, 0)))

    def to_heads(a):
        a = a.reshape(dec, bs, ATT_HEADS, ATT_HEAD_DIM).transpose(1, 2, 0, 3)
        return jnp.pad(a, ((0, 0), (0, 0), (0, TOK_ROWS - dec), (0, 0)))

    xp = x_prompt
    xs_tm = x_sample.transpose(1, 0, 2).reshape(dec * bs, d)
    outs = [[] for _ in range(8)]
    for l in range(depth):
        w_main = w_in[l, :, :MAIN_COLS].astype(BF16)
        w_small = jnp.zeros((d, LANES), F32).at[:, :2 * DN_HEADS].set(w_in[l, :, MAIN_COLS:])
        alog = _lane_row(a_log[l], DN_HEADS)
        dtb = _lane_row(dt_bias[l], DN_HEADS)
        w_out_bf = w_out[l].astype(BF16)
        nw = dn_norm_w[l].reshape(1, DN_HEAD_DIM)
        lg = ln_g[l].reshape(1, d)
        lb = ln_b[l].reshape(1, d)
        mod_p = mod_all[l, :bp].reshape(bp, 1, 3 * d)
        mod_s = mod_all[l, bp:nc_rows]

        (q_aug, k, v, k_aug, vt, kmean, sza, qd, kd, vd, szd, gb, nconv) = _in_prompt_call(
            xp, mod_p, w_main, w_small, conv_w[l], alog, dtb, cos_p, sin_p, tm)
        km_heads = kmean.reshape(bp, nb, ATT_HEADS, ATT_HEAD_DIM).transpose(0, 2, 1, 3)
        km_pad = jnp.pad(km_heads, ((0, 0), (0, 0), (ATT_HEAD_DIM, LANES - ATT_HEAD_DIM - nb),
                                    (0, LANES - ATT_HEAD_DIM)))
        oa = _moba_prompt_call(q_aug, k_aug, vt, km_pad)
        od, s_new = _gdn_call(qd, kd, vd, gb, zero_state, lambda n_: (n_, 0, 0, 0), DN_CHUNK)
        xp = _out_call(oa, sza, od, szd, xp, mod_p[:, :, 2 * d:], w_out_bf, nw, lg, lb, tm, alpha)
        outs[0].append(k.reshape(bp, seq, ATT_HEADS, ATT_HEAD_DIM))
        outs[1].append(v.reshape(bp, seq, ATT_HEADS, ATT_HEAD_DIM))
        outs[2].append(s_new)
        outs[3].append(nconv[:, SUBLANES - (CONV_WIDTH - 1):, :])

        mod_rows = jnp.tile(mod_s, (dec, 1))
        hist_tm = state_conv[l].transpose(1, 0, 2).reshape((CONV_WIDTH - 1) * bs, DN_QKV)
        (q, k, v, sza, qd, kd, vd, szd, gb, nconv) = _in_sample_call(
            xs_tm, mod_rows[:, :d], mod_rows[:, d:2 * d], w_main, w_small, conv_w[l], alog, dtb,
            cos_s, sin_s, hist_tm, bs, dec)
        k8, v8 = to_bm8(k), to_bm8(v)
        oh = _moba_sample_call(page_flat, to_heads(q), to_heads(k), to_heads(v), cache_kt, cache_vt,
                               l, npages, dec)
        oa8 = oh.transpose(0, 2, 1, 3).reshape(bs, SUBLANES, ATT_WIDTH)
        od8, s_new = _gdn_call(to_bm8(qd), to_bm8(kd), to_bm8(vd), to_bm8(gb), state_ssm,
                               lambda n_, l=l: (l, n_, 0, 0, 0), SUBLANES)
        rows8 = bs * SUBLANES
        flat = lambda a: a.reshape(1, rows8, a.shape[-1])
        gate_rows = jnp.repeat(mod_s[:, 2 * d:], SUBLANES, axis=0).reshape(1, rows8, d)
        xs8 = _out_call(flat(oa8), flat(to_bm8(sza)), flat(od8), flat(to_bm8(szd)), flat(to_bm8(xs_tm)),
                        gate_rows, w_out_bf, nw, lg, lb, rows8, alpha)
        xs_bm = xs8.reshape(bs, SUBLANES, d)[:, :dec]
        xs_tm = xs_bm.transpose(1, 0, 2).reshape(dec * bs, d)
        outs[4].append(k8[:, :dec].reshape(bs, dec, ATT_HEADS, ATT_HEAD_DIM))
        outs[5].append(v8[:, :dec].reshape(bs, dec, ATT_HEADS, ATT_HEAD_DIM))
        outs[6].append(s_new)
        outs[7].append(nconv.reshape(CONV_WIDTH - 1, bs, DN_QKV).transpose(1, 0, 2))

    stacked = [jnp.stack(o) for o in outs]
    return (xp, xs_bm, stacked[0], stacked[1], stacked[2], stacked[3],
            stacked[4], stacked[5], stacked[6], stacked[7])
```

```python
import functools
import math

import jax
import jax.numpy as jnp
from jax import lax
from jax.experimental import pallas as pl
from jax.experimental.pallas import tpu as pltpu

F32 = jnp.float32
BF16 = jnp.bfloat16
HIGHEST = lax.Precision.HIGHEST

ATT_HEADS = 8
ATT_HEAD_DIM = 64
ATT_WIDTH = ATT_HEADS * ATT_HEAD_DIM
DN_HEADS = 4
DN_HEAD_DIM = 128
DN_WIDTH = DN_HEADS * DN_HEAD_DIM
DN_QKV = 3 * DN_WIDTH
MAIN_COLS = 4 * ATT_WIDTH + DN_QKV + DN_WIDTH
MOBA_BLOCK = 256
MOBA_TOPK = 3
DN_CHUNK = 64
GDN_CHUNKS_PER_STEP = 2
CONV_WIDTH = 4
PAGE_SIZE = 128
ROPE_THETA = 10000.0
LN_EPS = 1e-5
RMS_EPS = 1e-6
L2_EPS = 1e-6
ATT_SCALE = ATT_HEAD_DIM ** -0.5
QK_SCALE = ATT_SCALE * math.log2(math.e)
NEG = -1e30
LANES = 128
SUBLANES = 8
TOK_ROWS = 2 * SUBLANES
VT_ROWS = ATT_HEAD_DIM + 16
MAX_MOBA_BLOCKS = LANES - ATT_HEAD_DIM
TILE_BLOCKS = 4
PAGES_PER_STEP = 16
VMEM_LIMIT = 56 * 1024 * 1024

NN_DIMS = (((1,), (0,)), ((), ()))
NT_DIMS = (((1,), (1,)), ((), ()))
TN_DIMS = (((0,), (0,)), ((), ()))


def _sigmoid(x):
    return 1.0 / (1.0 + jnp.exp(-x))


def _silu(x):
    return x * _sigmoid(x)


def _softplus(x):
    return jnp.maximum(x, 0.0) + jnp.log1p(jnp.exp(-jnp.abs(x)))


def _split_bf16(a):
    hi = a.astype(BF16)
    return hi, (a - hi.astype(F32)).astype(BF16)


def _mm3(a, b, dims=NN_DIMS):
    ah, al = _split_bf16(a)
    bh, bl = _split_bf16(b)

    def dot(x, y):
        return lax.dot_general(x, y, dims, preferred_element_type=F32)

    return (dot(ah, bl) + dot(al, bh)) + dot(ah, bh)


def _cparams(sem):
    return pltpu.CompilerParams(dimension_semantics=sem, vmem_limit_bytes=VMEM_LIMIT)


def _mod_kernel(c_ref, w_ref, b_ref, o_ref):
    s = _silu(c_ref[...])
    o_ref[0] = jnp.dot(s.astype(BF16), w_ref[0].astype(BF16), preferred_element_type=F32) + b_ref[0]


def _mod_call(c_all, w_ada, b_ada):
    depth, d, d3 = w_ada.shape
    rows = c_all.shape[0]
    nj = d3 // d
    return pl.pallas_call(
        _mod_kernel,
        grid=(depth, nj),
        in_specs=[
            pl.BlockSpec((rows, d), lambda l, j: (0, 0)),
            pl.BlockSpec((1, d, d), lambda l, j: (l, 0, j)),
            pl.BlockSpec((1, 1, d), lambda l, j: (l, 0, j)),
        ],
        out_specs=pl.BlockSpec((1, rows, d), lambda l, j: (l, 0, j)),
        out_shape=jax.ShapeDtypeStruct((depth, rows, d3), F32),
        compiler_params=_cparams(("arbitrary", "arbitrary")),
        name="mod",
    )(c_all, w_ada, b_ada.reshape(depth, 1, d3))


def _rope(p, cos, sin_signed):
    lane = lax.broadcasted_iota(jnp.int32, (1, LANES), 1)
    first_half = (lane % ATT_HEAD_DIM) < (ATT_HEAD_DIM // 2)
    outs = []
    for j in range(ATT_WIDTH // LANES):
        xj = p[:, j * LANES:(j + 1) * LANES]
        partner = jnp.where(first_half, pltpu.roll(xj, LANES - 32, 1), pltpu.roll(xj, 32, 1))
        outs.append(xj * cos + partner * sin_signed)
    return jnp.concatenate(outs, axis=-1)


def _l2norm_heads(c, scale):
    outs = []
    for h in range(DN_HEADS):
        ch = c[:, h * DN_HEAD_DIM:(h + 1) * DN_HEAD_DIM]
        ss = jnp.sum(ch * ch, axis=-1, keepdims=True)
        outs.append(ch * (lax.rsqrt(ss + L2_EPS) * scale))
    return jnp.concatenate(outs, axis=-1)


def _beta_decay(h, ws_ref, alog_ref, dtb_ref):
    raw = _mm3(h, ws_ref[...])
    lane = lax.broadcasted_iota(jnp.int32, (1, LANES), 1)
    beta = _sigmoid(raw)
    g = -jnp.exp(alog_ref[...]) * _softplus(raw + dtb_ref[...])
    return jnp.where(lane < DN_HEADS, beta, jnp.where(lane < 2 * DN_HEADS, g, 0.0))


def _proj(hb, w_ref, g):
    return jnp.dot(hb, w_ref[:, g * ATT_WIDTH:(g + 1) * ATT_WIDTH], preferred_element_type=F32)


def _head_slabs(p, fill):
    lane = lax.broadcasted_iota(jnp.int32, (1, LANES), 1)
    low = lane < ATT_HEAD_DIM
    outs = []
    for j in range(ATT_WIDTH // LANES):
        xj = p[:, j * LANES:(j + 1) * LANES]
        outs.append(jnp.where(low, xj, fill))
        outs.append(jnp.where(low, pltpu.roll(xj, ATT_HEAD_DIM, 1), fill))
    return jnp.concatenate(outs, axis=-1)


def _in_prompt_kernel(x_ref, mod_ref, w_ref, ws_ref, cw_ref, alog_ref, dtb_ref, cos_ref, sin_ref,
                      qa_ref, k_ref, v_ref, ka_ref, vt_ref, km_ref, sza_ref,
                      qd_ref, kd_ref, vd_ref, szd_ref, gb_ref, nc_ref, xbuf, *, tm):
    t = pl.program_id(1)
    d = x_ref.shape[-1]
    hist = SUBLANES

    @pl.when(t == 0)
    def _():
        xbuf[0:hist, :] = jnp.zeros((hist, DN_QKV), F32)

    x = x_ref[0]
    shift = mod_ref[0][:, 0:d]
    scale = mod_ref[0][:, d:2 * d]
    h = x * (1.0 + scale) + shift
    hb = h.astype(BF16)
    cos = cos_ref[...]
    sin = sin_ref[...]

    lane = lax.broadcasted_iota(jnp.int32, (1, LANES), 1)
    qa_ref[0] = _head_slabs(_rope(_proj(hb, w_ref, 0), cos, sin) * QK_SCALE, 0.0)
    kr = _rope(_proj(hb, w_ref, 1), cos, sin)
    k_ref[0] = kr
    block_onehot = (lane == ATT_HEAD_DIM + t).astype(F32)
    ka_ref[0] = _head_slabs(kr, block_onehot).astype(BF16)
    km_ref[0, 0] = jnp.sum(kr, axis=0, keepdims=True) * (1.0 / tm)
    v = _proj(hb, w_ref, 2)
    v_ref[0] = v
    vt = v.T
    for hd in range(ATT_HEADS):
        vt_ref[0, 0, hd, 0:ATT_HEAD_DIM, :] = vt[hd * ATT_HEAD_DIM:(hd + 1) * ATT_HEAD_DIM, :].astype(BF16)
        vt_ref[0, 0, hd, ATT_HEAD_DIM:VT_ROWS, :] = jnp.ones((VT_ROWS - ATT_HEAD_DIM, tm), BF16)
    sza_ref[0] = _silu(_proj(hb, w_ref, 3))
    szd_ref[0] = _silu(_proj(hb, w_ref, 7))
    gb_ref[0] = _beta_decay(h, ws_ref, alog_ref, dtb_ref)

    outs = (qd_ref, kd_ref, vd_ref)
    for gi in range(3):
        cols = slice(gi * DN_WIDTH, (gi + 1) * DN_WIDTH)
        xbuf[hist:hist + tm, cols] = _proj(hb, w_ref, 4 + gi)
        conv = jnp.zeros((tm, DN_WIDTH), F32)
        for i in range(CONV_WIDTH):
            off = hist - (CONV_WIDTH - 1) + i
            conv = conv + xbuf[off:off + tm, cols] * cw_ref[i:i + 1, cols]
        c = _silu(conv)
        if gi == 0:
            c = _l2norm_heads(c, DN_HEAD_DIM ** -0.5)
        elif gi == 1:
            c = _l2norm_heads(c, 1.0)
        outs[gi][0] = c

    tail = xbuf[tm:tm + hist, :]
    nc_ref[0] = tail
    xbuf[0:hist, :] = tail


def _in_prompt_call(x, mod, w_main, w_small, conv_w, alog, dtb, cos, sin, tm):
    n, t, d = x.shape
    nt = t // tm
    row = lambda n_, t_: (n_, t_, 0)
    full = lambda n_, t_: (0, 0)
    wide = jax.ShapeDtypeStruct((n, t, ATT_WIDTH), F32)
    blk = pl.BlockSpec((1, tm, ATT_WIDTH), row)
    slab_w = ATT_HEADS * LANES
    slab_blk = pl.BlockSpec((1, tm, slab_w), row)
    return pl.pallas_call(
        functools.partial(_in_prompt_kernel, tm=tm),
        grid=(n, nt),
        in_specs=[
            pl.BlockSpec((1, tm, d), row),
            pl.BlockSpec((1, 1, 3 * d), lambda n_, t_: (n_, 0, 0)),
            pl.BlockSpec((d, MAIN_COLS), full),
            pl.BlockSpec((d, LANES), full),
            pl.BlockSpec((CONV_WIDTH, DN_QKV), full),
            pl.BlockSpec((1, LANES), full),
            pl.BlockSpec((1, LANES), full),
            pl.BlockSpec((tm, LANES), lambda n_, t_: (t_, 0)),
            pl.BlockSpec((tm, LANES), lambda n_, t_: (t_, 0)),
        ],
        out_specs=[
            slab_blk, blk, blk, slab_blk,
            pl.BlockSpec((1, 1, ATT_HEADS, VT_ROWS, tm), lambda n_, t_: (n_, t_, 0, 0, 0)),
            pl.BlockSpec((1, 1, 1, ATT_WIDTH), lambda n_, t_: (n_, t_, 0, 0)),
            blk, blk, blk, blk, blk,
            pl.BlockSpec((1, tm, LANES), row),
            pl.BlockSpec((1, SUBLANES, DN_QKV), lambda n_, t_: (n_, 0, 0)),
        ],
        out_shape=[
            jax.ShapeDtypeStruct((n, t, slab_w), F32), wide, wide,
            jax.ShapeDtypeStruct((n, t, slab_w), BF16),
            jax.ShapeDtypeStruct((n, nt, ATT_HEADS, VT_ROWS, tm), BF16),
            jax.ShapeDtypeStruct((n, nt, 1, ATT_WIDTH), F32),
            wide, wide, wide, wide, wide,
            jax.ShapeDtypeStruct((n, t, LANES), F32),
            jax.ShapeDtypeStruct((n, SUBLANES, DN_QKV), F32),
        ],
        scratch_shapes=[pltpu.VMEM((tm + SUBLANES, DN_QKV), F32)],
        compiler_params=_cparams(("arbitrary", "arbitrary")),
        name="in_prompt",
    )(x, mod, w_main, w_small, conv_w, alog, dtb, cos, sin)


def _in_sample_kernel(x_ref, shift_ref, scale_ref, w_ref, ws_ref, cw_ref, alog_ref, dtb_ref,
                      cos_ref, sin_ref, hist_ref,
                      q_ref, k_ref, v_ref, sza_ref, qd_ref, kd_ref, vd_ref, szd_ref, gb_ref, nc_ref,
                      xall, *, batch, steps):
    rows = batch * steps
    nh = (CONV_WIDTH - 1) * batch
    h = x_ref[...] * (1.0 + scale_ref[...]) + shift_ref[...]
    hb = h.astype(BF16)
    cos = cos_ref[...]
    sin = sin_ref[...]
    q_ref[...] = _rope(_proj(hb, w_ref, 0), cos, sin) * QK_SCALE
    k_ref[...] = _rope(_proj(hb, w_ref, 1), cos, sin)
    v_ref[...] = _proj(hb, w_ref, 2)
    sza_ref[...] = _silu(_proj(hb, w_ref, 3))
    szd_ref[...] = _silu(_proj(hb, w_ref, 7))
    gb_ref[...] = _beta_decay(h, ws_ref, alog_ref, dtb_ref)

    xall[0:nh, :] = hist_ref[...]
    outs = (qd_ref, kd_ref, vd_ref)
    for gi in range(3):
        cols = slice(gi * DN_WIDTH, (gi + 1) * DN_WIDTH)
        xall[nh:nh + rows, cols] = _proj(hb, w_ref, 4 + gi)
        conv = jnp.zeros((rows, DN_WIDTH), F32)
        for i in range(CONV_WIDTH):
            conv = conv + xall[i * batch:i * batch + rows, cols] * cw_ref[i:i + 1, cols]
        c = _silu(conv)
        if gi == 0:
            c = _l2norm_heads(c, DN_HEAD_DIM ** -0.5)
        elif gi == 1:
            c = _l2norm_heads(c, 1.0)
        outs[gi][...] = c
    nc_ref[...] = xall[rows:rows + nh, :]


def _in_sample_call(x_tm, shift, scale, w_main, w_small, conv_w, alog, dtb, cos, sin, hist_tm, batch, steps):
    rows = batch * steps
    nh = (CONV_WIDTH - 1) * batch
    wide = jax.ShapeDtypeStruct((rows, ATT_WIDTH), F32)
    return pl.pallas_call(
        functools.partial(_in_sample_kernel, batch=batch, steps=steps),
        out_shape=[wide] * 8 + [jax.ShapeDtypeStruct((rows, LANES), F32),
                                jax.ShapeDtypeStruct((nh, DN_QKV), F32)],
        scratch_shapes=[pltpu.VMEM((rows + nh, DN_QKV), F32)],
        compiler_params=pltpu.CompilerParams(vmem_limit_bytes=VMEM_LIMIT),
        name="in_sample",
    )(x_tm, shift, scale, w_main, w_small, conv_w, alog, dtb, cos, sin, hist_tm)


def _topk_bias(gate, valid, blkf, axis=-1):
    gate = jnp.where(valid, gate, -jnp.inf)
    bias = jnp.full(gate.shape, NEG, F32)
    for _ in range(MOBA_TOPK):
        mx = jnp.max(gate, axis=axis, keepdims=True)
        cand = (gate == mx) & (mx > -jnp.inf)
        idx = jnp.min(jnp.where(cand, blkf, 1e9), axis=axis, keepdims=True)
        pick = blkf == idx
        bias = jnp.where(pick, 0.0, bias)
        gate = jnp.where(pick, -jnp.inf, gate)
    return bias


def _moba_prompt_kernel(q_ref, k_ref, vt_ref, km_ref, o_ref, s_ref, m_ref, acc_ref):
    i = pl.program_id(2)
    bs = MOBA_BLOCK
    lane = lax.broadcasted_iota(jnp.int32, (1, LANES), 1)
    low = lane < ATT_HEAD_DIM
    brow = lax.broadcasted_iota(jnp.int32, (MAX_MOBA_BLOCKS, bs), 0)
    browf = brow.astype(F32)
    tk = TILE_BLOCKS * bs
    tl = i // TILE_BLOCKS
    krow = lax.broadcasted_iota(jnp.int32, (tk, bs), 0)
    qcol = lax.broadcasted_iota(jnp.int32, (tk, bs), 1)

    def pv(tile, hh, p):
        pb = p.astype(BF16)
        out = None
        for b in range(TILE_BLOCKS):
            part = jnp.dot(vt_ref[0, tile * TILE_BLOCKS + b, hh], pb[b * bs:(b + 1) * bs],
                           preferred_element_type=F32)
            out = part if out is None else out + part
        return out

    qas = []
    for hh in range(2):
        qs = q_ref[0][:, hh * LANES:(hh + 1) * LANES]
        gate_t = _mm3(km_ref[0, hh, ATT_HEAD_DIM:, :], qs, NT_DIMS)
        bias = _topk_bias(gate_t, brow < i, browf, axis=0)
        bias = jnp.where(brow == i, 0.0, bias)
        bias_t = jnp.concatenate([jnp.full((ATT_HEAD_DIM, bs), NEG, F32), bias], axis=0)
        qas.append(jnp.where(low, qs, bias_t.T).astype(BF16))

    def scores(tile):
        off = pl.multiple_of(tile * tk, tk)
        return [lax.dot_general(k_ref[0, pl.ds(off, tk), hh * LANES:(hh + 1) * LANES], qas[hh],
                                NT_DIMS, preferred_element_type=F32) for hh in range(2)]

    causal = (krow + tl * tk) <= (qcol + i * bs)
    for hh, s in enumerate(scores(tl)):
        s = jnp.where(causal, s, NEG)
        m = jnp.max(s, axis=0, keepdims=True)
        m_ref[hh] = m
        acc_ref[hh] = pv(tl, hh, jnp.exp2(s - m))
    for hh, s in enumerate(scores(jnp.maximum(tl - 1, 0))):
        s_ref[0, hh] = s

    def body(t, carry):
        j = tl - 1 - t
        slot = t % 2
        cur = [s_ref[slot, hh] for hh in range(2)]
        nxt = scores(jnp.maximum(j - 1, 0))
        for hh in range(2):
            m = m_ref[hh]
            m_new = jnp.maximum(m, jnp.max(cur[hh], axis=0, keepdims=True))
            acc_ref[hh] = jnp.exp2(m - m_new) * acc_ref[hh] + pv(j, hh, jnp.exp2(cur[hh] - m_new))
            m_ref[hh] = m_new
        for hh in range(2):
            s_ref[1 - slot, hh] = nxt[hh]
        return carry

    lax.fori_loop(0, tl, body, 0)
    d = ATT_HEAD_DIM
    a0, a1 = acc_ref[0], acc_ref[1]
    o = jnp.concatenate([a0[0:d] / a0[d:d + 1], a1[0:d] / a1[d:d + 1]], axis=0)
    o_ref[0] = o.T


def _moba_prompt_call(q_aug, k_aug, vt, km_pad):
    n, t, _ = q_aug.shape
    nb = t // MOBA_BLOCK
    npair = ATT_WIDTH // LANES
    pair_w = 2 * LANES
    return pl.pallas_call(
        _moba_prompt_kernel,
        grid=(n, npair, nb),
        in_specs=[
            pl.BlockSpec((1, MOBA_BLOCK, pair_w), lambda n_, p_, i_: (n_, i_, p_)),
            pl.BlockSpec((1, t, pair_w), lambda n_, p_, i_: (n_, 0, p_)),
            pl.BlockSpec((1, nb, 2, VT_ROWS, MOBA_BLOCK), lambda n_, p_, i_: (n_, 0, p_, 0, 0)),
            pl.BlockSpec((1, 2, LANES, LANES), lambda n_, p_, i_: (n_, p_, 0, 0)),
        ],
        out_specs=pl.BlockSpec((1, MOBA_BLOCK, LANES), lambda n_, p_, i_: (n_, i_, p_)),
        out_shape=jax.ShapeDtypeStruct((n, t, ATT_WIDTH), F32),
        scratch_shapes=[
            pltpu.VMEM((2, 2, TILE_BLOCKS * MOBA_BLOCK, MOBA_BLOCK), F32),
            pltpu.VMEM((2, 1, MOBA_BLOCK), F32),
            pltpu.VMEM((2, VT_ROWS, MOBA_BLOCK), F32),
        ],
        compiler_params=_cparams(("arbitrary", "arbitrary", "arbitrary")),
        name="moba_prompt",
    )(q_aug, k_aug, vt, km_pad)


def _moba_sample_kernel(pt_ref, q_ref, kn_ref, vn_ref, *rest, npages, steps):
    pps = PAGES_PER_STEP
    kp = rest[:pps]
    vp = rest[pps:2 * pps]
    o_ref = rest[2 * pps]
    lg_ref, km_ref, acc_ref, lsum_ref = rest[2 * pps + 1:]
    s = pl.program_id(1)
    nsteps = npages // pps
    nblk = npages * PAGE_SIZE // MOBA_BLOCK
    ppb = MOBA_BLOCK // PAGE_SIZE
    hs = range(ATT_HEADS)
    half = TOK_ROWS // 2

    lane = lax.broadcasted_iota(jnp.int32, (1, LANES), 1)

    @pl.when(s == 0)
    def _():
        km_ref[...] = jnp.zeros(km_ref.shape, F32)

    @pl.when(s < nsteps)
    def _():
        qb = [q_ref[0, h].astype(BF16) for h in hs]
        ksum = [None] * ATT_HEADS
        for i in range(pps):
            page = s * pps + i
            for h in hs:
                kt = kp[i][h]
                lg = jnp.dot(qb[h], kt.astype(BF16), preferred_element_type=F32)
                lg_ref[page, h] = lg[0:half]
                ksum[h] = kt if i % ppb == 0 else ksum[h] + kt
                if i % ppb == ppb - 1:
                    col = jnp.sum(ksum[h], axis=-1, keepdims=True) * (1.0 / MOBA_BLOCK)
                    km_ref[h] = jnp.where(lane == s * (pps // ppb) + i // ppb, col, km_ref[h])

    @pl.when(s == nsteps - 1)
    def _():
        rows = ATT_HEADS * half
        blk = lax.broadcasted_iota(jnp.int32, (rows, LANES), 1)
        trow = lax.broadcasted_iota(jnp.int32, (rows, TOK_ROWS), 0) % half
        tcol = lax.broadcasted_iota(jnp.int32, (rows, TOK_ROWS), 1)
        qf = [q_ref[0, h] for h in hs]
        gate = jnp.concatenate(
            [jnp.dot(qf[h][0:half], km_ref[h], precision=HIGHEST, preferred_element_type=F32)
             for h in hs], axis=0)
        bias = _topk_bias(gate, blk < nblk, blk.astype(F32))
        lo = jnp.concatenate(
            [lax.dot_general(qf[h].astype(BF16), kn_ref[0, h].astype(BF16), NT_DIMS,
                             preferred_element_type=F32)[0:half] for h in hs], axis=0)
        lo = jnp.where((tcol <= trow) & (tcol < steps), lo, NEG)
        bcols = [[bias[h * half:(h + 1) * half, b:b + 1] for b in range(nblk)] for h in hs]
        mts = []
        for h in hs:
            mt = jnp.full((half, PAGE_SIZE), NEG, F32)
            for b in range(nblk):
                for pg in range(ppb):
                    mt = jnp.maximum(mt, lg_ref[b * ppb + pg, h] + bcols[h][b])
            mts.append(mt)
        m = jnp.maximum(jnp.max(jnp.concatenate(mts, axis=0), axis=-1, keepdims=True),
                        jnp.max(lo, axis=-1, keepdims=True))
        po = jnp.exp2(lo - m)
        lts = []
        for h in hs:
            mh = m[h * half:(h + 1) * half]
            lt = jnp.zeros((half, PAGE_SIZE), F32)
            for b in range(nblk):
                for pg in range(ppb):
                    pb = jnp.exp2(lg_ref[b * ppb + pg, h] + bcols[h][b] - mh)
                    lg_ref[b * ppb + pg, h] = pb
                    lt = lt + pb
            lts.append(lt)
        lsum = (jnp.sum(jnp.concatenate(lts, axis=0), axis=-1, keepdims=True)
                + jnp.sum(po, axis=-1, keepdims=True))
        zeros = jnp.zeros((half, TOK_ROWS), F32)
        for h in hs:
            lsum_ref[h] = lsum[h * half:(h + 1) * half]
            po16 = jnp.concatenate([po[h * half:(h + 1) * half], zeros], axis=0).astype(BF16)
            acc_ref[h] = jnp.dot(po16, vn_ref[0, h].astype(BF16), preferred_element_type=F32)[0:half]

    @pl.when(s >= nsteps)
    def _():
        acc = [acc_ref[h] for h in hs]
        zeros = jnp.zeros((half, PAGE_SIZE), F32)
        for i in range(pps):
            page = (s - nsteps) * pps + i
            for h in hs:
                p16 = jnp.concatenate([lg_ref[page, h], zeros], axis=0).astype(BF16)
                vt = vp[i][h].astype(BF16)
                acc[h] = acc[h] + lax.dot_general(p16, vt, NT_DIMS, preferred_element_type=F32)[0:half]
        for h in hs:
            acc_ref[h] = acc[h]

    @pl.when(s == 2 * nsteps - 1)
    def _():
        for h in hs:
            o_ref[0, h] = acc_ref[h] / lsum_ref[h]


def _moba_sample_call(page_flat, qh, knh, vnh, cache_k4, cache_v4, layer, npages, steps):
    batch = qh.shape[0]
    pps = PAGES_PER_STEP
    nsteps = npages // pps
    half = TOK_ROWS // 2
    page_block = (None, None, ATT_HEADS, ATT_HEAD_DIM, PAGE_SIZE)
    tok = pl.BlockSpec((1, ATT_HEADS, TOK_ROWS, ATT_HEAD_DIM), lambda n_, s_, pt: (n_, 0, 0, 0))

    def kspec(i):
        return pl.BlockSpec(
            page_block,
            lambda n_, s_, pt: (layer, pt[n_ * npages + jnp.minimum(s_, nsteps - 1) * pps + i], 0, 0, 0))

    def vspec(i):
        return pl.BlockSpec(
            page_block,
            lambda n_, s_, pt: (layer, pt[n_ * npages + jnp.maximum(s_ - nsteps, 0) * pps + i], 0, 0, 0))

    grid_spec = pltpu.PrefetchScalarGridSpec(
        num_scalar_prefetch=1,
        grid=(batch, 2 * nsteps),
        in_specs=[tok, tok, tok] + [kspec(i) for i in range(pps)] + [vspec(i) for i in range(pps)],
        out_specs=pl.BlockSpec((1, ATT_HEADS, half, ATT_HEAD_DIM), lambda n_, s_, pt: (n_, 0, 0, 0)),
        scratch_shapes=[
            pltpu.VMEM((npages, ATT_HEADS, half, PAGE_SIZE), F32),
            pltpu.VMEM((ATT_HEADS, ATT_HEAD_DIM, LANES), F32),
            pltpu.VMEM((ATT_HEADS, half, ATT_HEAD_DIM), F32),
            pltpu.VMEM((ATT_HEADS, half, 1), F32),
        ],
    )
    return pl.pallas_call(
        functools.partial(_moba_sample_kernel, npages=npages, steps=steps),
        grid_spec=grid_spec,
        out_shape=jax.ShapeDtypeStruct((batch, ATT_HEADS, half, ATT_HEAD_DIM), F32),
        compiler_params=_cparams(("arbitrary", "arbitrary")),
        name="moba_sample",
    )(page_flat, qh, knh, vnh, *([cache_k4] * pps), *([cache_v4] * pps))


def _gdn_kernel(q_ref, k_ref, v_ref, gb_ref, s0_ref, o_ref, sout_ref, s_scr, *, chunk, cps):
    c = pl.program_id(1)
    nc = pl.num_programs(1)

    @pl.when(c == 0)
    def _():
        s_scr[...] = s0_ref[0]

    ri = lax.broadcasted_iota(jnp.int32, (chunk, chunk), 0)
    ci = lax.broadcasted_iota(jnp.int32, (chunk, chunk), 1)
    tril = ri >= ci
    stril = ri > ci
    eye = (ri == ci).astype(F32)
    nsq = int(round(math.log2(chunk))) - 1

    def mm1(a, b, dims=NN_DIMS):
        return lax.dot_general(a.astype(BF16), b.astype(BF16), dims, preferred_element_type=F32)

    hs = range(DN_HEADS)
    cols = [slice(h * DN_HEAD_DIM, (h + 1) * DN_HEAD_DIM) for h in hs]
    items = [(cc, h) for cc in range(cps) for h in hs]
    rows = [slice(cc * chunk, (cc + 1) * chunk) for cc in range(cps)]
    gb = [gb_ref[0, rows[cc], :] for cc in range(cps)]
    gcum = [jnp.dot(tril.astype(F32), gb[cc], precision=HIGHEST, preferred_element_type=F32)
            for cc in range(cps)]
    gcum_rows = [lax.dot_general(gcum[cc], eye, TN_DIMS, precision=HIGHEST, preferred_element_type=F32)
                 for cc in range(cps)]
    q = {it: q_ref[0, rows[it[0]], cols[it[1]]] for it in items}
    k = {it: k_ref[0, rows[it[0]], cols[it[1]]] for it in items}
    v = {it: v_ref[0, rows[it[0]], cols[it[1]]] for it in items}
    beta = {(cc, h): gb[cc][:, h:h + 1] for cc, h in items}
    gc = {(cc, h): gcum[cc][:, DN_HEADS + h:DN_HEADS + h + 1] for cc, h in items}
    kbeta = {it: k[it] * beta[it] for it in items}
    kk = {it: _mm3(kbeta[it], k[it], NT_DIMS) for it in items}
    qk = {it: mm1(q[it], k[it], NT_DIMS) for it in items}
    decay = {}
    for cc, h in items:
        diff = gc[cc, h] - gcum_rows[cc][DN_HEADS + h:DN_HEADS + h + 1, :]
        decay[cc, h] = jnp.where(tril, jnp.exp(jnp.where(tril, diff, 0.0)), 0.0)
    eg = {it: jnp.exp(gc[it]) for it in items}
    pw = {it: -jnp.where(stril, kk[it] * decay[it], 0.0) for it in items}
    y = {it: jnp.concatenate([v[it] * beta[it], kbeta[it] * eg[it]], axis=-1) for it in items}
    y = {it: y[it] + _mm3(pw[it], y[it]) for it in items}
    for _ in range(nsq):
        pw = {it: _mm3(pw[it], pw[it]) for it in items}
        y = {it: y[it] + _mm3(pw[it], y[it]) for it in items}

    state = [s_scr[h] for h in hs]
    for cc in range(cps):
        its = [(cc, h) for h in hs]
        v_new = [y[it][:, 0:DN_HEAD_DIM] - _mm3(y[it][:, DN_HEAD_DIM:2 * DN_HEAD_DIM], state[it[1]])
                 for it in its]
        g_last = [gc[it][chunk - 1:chunk, :] for it in its]
        kv = [_mm3(k[it] * jnp.exp(g_last[h] - gc[it]), v_new[h], TN_DIMS) for h, it in enumerate(its)]
        for h, it in enumerate(its):
            attn = jnp.where(tril, qk[it] * decay[it], 0.0)
            o_ref[0, rows[cc], cols[h]] = mm1(q[it] * eg[it], state[h]) + mm1(attn, v_new[h])
        state = [state[h] * jnp.exp(g_last[h]) + kv[h] for h in hs]
    for h in hs:
        s_scr[h] = state[h]

    @pl.when(c == nc - 1)
    def _():
        sout_ref[0] = s_scr[...]


def _gdn_call(qd, kd, vd, gb, s0, s0_index, chunk, cps):
    n, t, _ = qd.shape
    step_rows = chunk * cps
    nc = t // step_rows
    row = lambda n_, c_: (n_, c_, 0)
    blk = pl.BlockSpec((1, step_rows, DN_WIDTH), row)
    sshape = (DN_HEADS, DN_HEAD_DIM, DN_HEAD_DIM)
    s0_block = (None,) * (s0.ndim - 4) + (1,) + sshape
    return pl.pallas_call(
        functools.partial(_gdn_kernel, chunk=chunk, cps=cps),
        grid=(n, nc),
        in_specs=[blk, blk, blk,
                  pl.BlockSpec((1, step_rows, LANES), row),
                  pl.BlockSpec(s0_block, lambda n_, c_: s0_index(n_))],
        out_specs=[blk, pl.BlockSpec((1,) + sshape, lambda n_, c_: (n_, 0, 0, 0))],
        out_shape=[jax.ShapeDtypeStruct((n, t, DN_WIDTH), F32),
                   jax.ShapeDtypeStruct((n,) + sshape, F32)],
        scratch_shapes=[pltpu.VMEM(sshape, F32)],
        compiler_params=_cparams(("arbitrary", "arbitrary")),
        name="gdn",
    )(qd, kd, vd, gb, s0)


def _out_kernel(oa_ref, sza_ref, od_ref, szd_ref, x_ref, gate_ref, wo_ref, nw_ref, lg_ref, lb_ref,
                o_ref, *, alpha):
    ya = (oa_ref[0] * sza_ref[0]).astype(BF16)
    od = od_ref[0]
    parts = []
    for h in range(DN_HEADS):
        oh = od[:, h * DN_HEAD_DIM:(h + 1) * DN_HEAD_DIM]
        ms = jnp.mean(oh * oh, axis=-1, keepdims=True)
        parts.append(oh * lax.rsqrt(ms + RMS_EPS) * nw_ref[...])
    yd = (jnp.concatenate(parts, axis=-1) * szd_ref[0]).astype(BF16)
    y = (jnp.dot(ya, wo_ref[0:ATT_WIDTH, :], preferred_element_type=F32)
         + jnp.dot(yd, wo_ref[ATT_WIDTH:ATT_WIDTH + DN_WIDTH, :], preferred_element_type=F32))
    r = alpha * x_ref[0] + (1.0 + gate_ref[0]) * y
    mu = jnp.mean(r, axis=-1, keepdims=True)
    rc = r - mu
    var = jnp.mean(rc * rc, axis=-1, keepdims=True)
    o_ref[0] = rc * lax.rsqrt(var + LN_EPS) * lg_ref[...] + lb_ref[...]


def _out_call(oa, sza, od, szd, x, gate, w_out_bf, nw, lg, lb, tm, alpha):
    n, t, d = x.shape
    nt = t // tm
    row = lambda n_, t_: (n_, t_, 0)
    full = lambda n_, t_: (0, 0)
    blk = pl.BlockSpec((1, tm, ATT_WIDTH), row)
    if gate.shape[1] == 1:
        gate_spec = pl.BlockSpec((1, 1, d), lambda n_, t_: (n_, 0, 0))
    else:
        gate_spec = pl.BlockSpec((1, tm, d), row)
    return pl.pallas_call(
        functools.partial(_out_kernel, alpha=alpha),
        grid=(n, nt),
        in_specs=[blk, blk, blk, blk,
                  pl.BlockSpec((1, tm, d), row),
                  gate_spec,
                  pl.BlockSpec((ATT_WIDTH + DN_WIDTH, d), full),
                  pl.BlockSpec((1, DN_HEAD_DIM), full),
                  pl.BlockSpec((1, d), full),
                  pl.BlockSpec((1, d), full)],
        out_specs=pl.BlockSpec((1, tm, d), row),
        out_shape=jax.ShapeDtypeStruct((n, t, d), F32),
        compiler_params=_cparams(("arbitrary", "arbitrary")),
        name="out_proj",
    )(oa, sza, od, szd, x, gate, w_out_bf, nw, lg, lb)


def _rope_tables(pos):
    half = ATT_HEAD_DIM // 2
    inv_freq = ROPE_THETA ** (-jnp.arange(half, dtype=F32) / half)
    ang = pos.astype(F32)[:, None] * inv_freq[None, :]
    cos = jnp.cos(ang)
    sin = jnp.sin(ang)
    reps = LANES // ATT_HEAD_DIM
    return jnp.tile(cos, (1, 2 * reps)), jnp.tile(jnp.concatenate([-sin, sin], axis=1), (1, reps))


def _lane_row(vec, offset):
    return jnp.zeros((1, LANES), F32).at[0, offset:offset + vec.shape[0]].set(vec.astype(F32))


def kernel(x_prompt, x_sample, c_prompt, c_sample, cache_k, cache_v, page_table, state_ssm, state_conv,
           w_ada, b_ada, w_in, conv_w, a_log, dt_bias, dn_norm_w, w_out, ln_g, ln_b):
    depth, d, _ = w_ada.shape
    bp, seq, _ = x_prompt.shape
    bs, dec, _ = x_sample.shape
    npages = page_table.shape[1]
    past = npages * PAGE_SIZE
    n_pool = cache_k.shape[1]
    alpha = float((2 * depth) ** 0.25)
    tm = MOBA_BLOCK
    nb = seq // tm
    assert seq % tm == 0 and past % MOBA_BLOCK == 0 and nb <= MAX_MOBA_BLOCKS and nb % TILE_BLOCKS == 0
    assert seq % (DN_CHUNK * GDN_CHUNKS_PER_STEP) == 0
    assert dec <= SUBLANES and npages % PAGES_PER_STEP == 0

    nc_rows = bp + bs
    c_rows = -(-nc_rows // SUBLANES) * SUBLANES
    c_all = jnp.zeros((c_rows, d), F32).at[:bp].set(c_prompt).at[bp:nc_rows].set(c_sample)
    mod_all = _mod_call(c_all, w_ada, b_ada)

    cos_p, sin_p = _rope_tables(jnp.arange(seq, dtype=jnp.int32))
    pos_s = past + jnp.repeat(jnp.arange(dec, dtype=jnp.int32), bs)
    cos_s, sin_s = _rope_tables(pos_s)

    cache_kt = cache_k.transpose(0, 1, 3, 4, 2)
    cache_vt = cache_v.transpose(0, 1, 3, 4, 2)
    page_flat = page_table.reshape(-1).astype(jnp.int32)
    zero_state = jnp.zeros((bp, DN_HEADS, DN_HEAD_DIM, DN_HEAD_DIM), F32)

    def to_bm8(a):
        w = a.shape[-1]
        a = a.reshape(dec, bs, w).transpose(1, 0, 2)
        return jnp.pad(a, ((0, 0), (0, SUBLANES - dec), (0, 0)))

    def to_heads(a):
        a = a.reshape(dec, bs, ATT_HEADS, ATT_HEAD_DIM).transpose(1, 2, 0, 3)
        return jnp.pad(a, ((0, 0), (0, 0), (0, TOK_ROWS - dec), (0, 0)))

    xp = x_prompt
    xs_tm = x_sample.transpose(1, 0, 2).reshape(dec * bs, d)
    outs = [[] for _ in range(8)]
    for l in range(depth):
        w_main = w_in[l, :, :MAIN_COLS].astype(BF16)
        w_small = jnp.zeros((d, LANES), F32).at[:, :2 * DN_HEADS].set(w_in[l, :, MAIN_COLS:])
        alog = _lane_row(a_log[l], DN_HEADS)
        dtb = _lane_row(dt_bias[l], DN_HEADS)
        w_out_bf = w_out[l].astype(BF16)
        nw = dn_norm_w[l].reshape(1, DN_HEAD_DIM)
        lg = ln_g[l].reshape(1, d)
        lb = ln_b[l].reshape(1, d)
        mod_p = mod_all[l, :bp].reshape(bp, 1, 3 * d)
        mod_s = mod_all[l, bp:nc_rows]

        (q_aug, k, v, k_aug, vt, kmean, sza, qd, kd, vd, szd, gb, nconv) = _in_prompt_call(
            xp, mod_p, w_main, w_small, conv_w[l], alog, dtb, cos_p, sin_p, tm)
        km_heads = kmean.reshape(bp, nb, ATT_HEADS, ATT_HEAD_DIM).transpose(0, 2, 1, 3)
        km_pad = jnp.pad(km_heads, ((0, 0), (0, 0), (ATT_HEAD_DIM, LANES - ATT_HEAD_DIM - nb),
                                    (0, LANES - ATT_HEAD_DIM)))
        oa = _moba_prompt_call(q_aug, k_aug, vt, km_pad)
        od, s_new = _gdn_call(qd, kd, vd, gb, zero_state, lambda n_: (n_, 0, 0, 0), DN_CHUNK,
                              GDN_CHUNKS_PER_STEP)
        xp = _out_call(oa, sza, od, szd, xp, mod_p[:, :, 2 * d:], w_out_bf, nw, lg, lb, tm, alpha)
        outs[0].append(k.reshape(bp, seq, ATT_HEADS, ATT_HEAD_DIM))
        outs[1].append(v.reshape(bp, seq, ATT_HEADS, ATT_HEAD_DIM))
        outs[2].append(s_new)
        outs[3].append(nconv[:, SUBLANES - (CONV_WIDTH - 1):, :])

        mod_rows = jnp.tile(mod_s, (dec, 1))
        hist_tm = state_conv[l].transpose(1, 0, 2).reshape((CONV_WIDTH - 1) * bs, DN_QKV)
        (q, k, v, sza, qd, kd, vd, szd, gb, nconv) = _in_sample_call(
            xs_tm, mod_rows[:, :d], mod_rows[:, d:2 * d], w_main, w_small, conv_w[l], alog, dtb,
            cos_s, sin_s, hist_tm, bs, dec)
        k8, v8 = to_bm8(k), to_bm8(v)
        oh = _moba_sample_call(page_flat, to_heads(q), to_heads(k), to_heads(v), cache_kt, cache_vt,
                               l, npages, dec)
        oa8 = oh.transpose(0, 2, 1, 3).reshape(bs, SUBLANES, ATT_WIDTH)
        od8, s_new = _gdn_call(to_bm8(qd), to_bm8(kd), to_bm8(vd), to_bm8(gb), state_ssm,
                               lambda n_, l=l: (l, n_, 0, 0, 0), SUBLANES, 1)
        rows8 = bs * SUBLANES
        flat = lambda a: a.reshape(1, rows8, a.shape[-1])
        gate_rows = jnp.repeat(mod_s[:, 2 * d:], SUBLANES, axis=0).reshape(1, rows8, d)
        xs8 = _out_call(flat(oa8), flat(to_bm8(sza)), flat(od8), flat(to_bm8(szd)), flat(to_bm8(xs_tm)),
                        gate_rows, w_out_bf, nw, lg, lb, rows8, alpha)
        xs_bm = xs8.reshape(bs, SUBLANES, d)[:, :dec]
        xs_tm = xs_bm.transpose(1, 0, 2).reshape(dec * bs, d)
        outs[4].append(k8[:, :dec].reshape(bs, dec, ATT_HEADS, ATT_HEAD_DIM))
        outs[5].append(v8[:, :dec].reshape(bs, dec, ATT_HEADS, ATT_HEAD_DIM))
        outs[6].append(s_new)
        outs[7].append(nconv.reshape(CONV_WIDTH - 1, bs, DN_QKV).transpose(1, 0, 2))

    stacked = [jnp.stack(o) for o in outs]
    return (xp, xs_bm, stacked[0], stacked[1], stacked[2], stacked[3],
            stacked[4], stacked[5], stacked[6], stacked[7])
```

```python
import functools
import math

import jax
import jax.numpy as jnp
from jax import lax
from jax.experimental import pallas as pl
from jax.experimental.pallas import tpu as pltpu

F32 = jnp.float32
BF16 = jnp.bfloat16
HIGHEST = lax.Precision.HIGHEST

ATT_HEADS = 8
ATT_HEAD_DIM = 64
ATT_WIDTH = ATT_HEADS * ATT_HEAD_DIM
DN_HEADS = 4
DN_HEAD_DIM = 128
DN_WIDTH = DN_HEADS * DN_HEAD_DIM
DN_QKV = 3 * DN_WIDTH
MAIN_COLS = 4 * ATT_WIDTH + DN_QKV + DN_WIDTH
MOBA_BLOCK = 256
MOBA_TOPK = 3
DN_CHUNK = 64
GDN_CHUNKS_PER_STEP = 4
CONV_WIDTH = 4
PAGE_SIZE = 128
ROPE_THETA = 10000.0
LN_EPS = 1e-5
RMS_EPS = 1e-6
L2_EPS = 1e-6
ATT_SCALE = ATT_HEAD_DIM ** -0.5
QK_SCALE = ATT_SCALE * math.log2(math.e)
NEG = -1e30
LANES = 128
SUBLANES = 8
TOK_ROWS = 2 * SUBLANES
VT_ROWS = ATT_HEAD_DIM + 16
MAX_MOBA_BLOCKS = LANES - ATT_HEAD_DIM
TILE_BLOCKS = 4
Q_BLOCKS = 1
assert TILE_BLOCKS % Q_BLOCKS == 0
PAGES_PER_STEP = 16
VMEM_LIMIT = 56 * 1024 * 1024

NN_DIMS = (((1,), (0,)), ((), ()))
NT_DIMS = (((1,), (1,)), ((), ()))
TN_DIMS = (((0,), (0,)), ((), ()))


def _sigmoid(x):
    return 1.0 / (1.0 + jnp.exp(-x))


def _silu(x):
    return x * _sigmoid(x)


def _softplus(x):
    return jnp.maximum(x, 0.0) + jnp.log1p(jnp.exp(-jnp.abs(x)))


def _split_bf16(a):
    hi = a.astype(BF16)
    return hi, (a - hi.astype(F32)).astype(BF16)


def _mm3(a, b, dims=NN_DIMS):
    ah, al = _split_bf16(a)
    bh, bl = _split_bf16(b)

    def dot(x, y):
        return lax.dot_general(x, y, dims, preferred_element_type=F32)

    return (dot(ah, bl) + dot(al, bh)) + dot(ah, bh)


def _cparams(sem):
    return pltpu.CompilerParams(dimension_semantics=sem, vmem_limit_bytes=VMEM_LIMIT)


def _mod_kernel(c_ref, w_ref, b_ref, o_ref):
    s = _silu(c_ref[...])
    o_ref[0] = jnp.dot(s.astype(BF16), w_ref[0].astype(BF16), preferred_element_type=F32) + b_ref[0]


def _mod_call(c_all, w_ada, b_ada):
    depth, d, d3 = w_ada.shape
    rows = c_all.shape[0]
    nj = d3 // d
    return pl.pallas_call(
        _mod_kernel,
        grid=(depth, nj),
        in_specs=[
            pl.BlockSpec((rows, d), lambda l, j: (0, 0)),
            pl.BlockSpec((1, d, d), lambda l, j: (l, 0, j)),
            pl.BlockSpec((1, 1, d), lambda l, j: (l, 0, j)),
        ],
        out_specs=pl.BlockSpec((1, rows, d), lambda l, j: (l, 0, j)),
        out_shape=jax.ShapeDtypeStruct((depth, rows, d3), F32),
        compiler_params=_cparams(("arbitrary", "arbitrary")),
        name="mod",
    )(c_all, w_ada, b_ada.reshape(depth, 1, d3))


def _rope(p, cos, sin_signed):
    lane = lax.broadcasted_iota(jnp.int32, (1, LANES), 1)
    first_half = (lane % ATT_HEAD_DIM) < (ATT_HEAD_DIM // 2)
    outs = []
    for j in range(ATT_WIDTH // LANES):
        xj = p[:, j * LANES:(j + 1) * LANES]
        partner = jnp.where(first_half, pltpu.roll(xj, LANES - 32, 1), pltpu.roll(xj, 32, 1))
        outs.append(xj * cos + partner * sin_signed)
    return jnp.concatenate(outs, axis=-1)


def _l2norm_heads(c, scale):
    outs = []
    for h in range(DN_HEADS):
        ch = c[:, h * DN_HEAD_DIM:(h + 1) * DN_HEAD_DIM]
        ss = jnp.sum(ch * ch, axis=-1, keepdims=True)
        outs.append(ch * (lax.rsqrt(ss + L2_EPS) * scale))
    return jnp.concatenate(outs, axis=-1)


def _beta_decay(h, ws_ref, alog_ref, dtb_ref):
    raw = _mm3(h, ws_ref[...])
    lane = lax.broadcasted_iota(jnp.int32, (1, LANES), 1)
    beta = _sigmoid(raw)
    g = -jnp.exp(alog_ref[...]) * _softplus(raw + dtb_ref[...])
    return jnp.where(lane < DN_HEADS, beta, jnp.where(lane < 2 * DN_HEADS, g, 0.0))


def _proj(hb, w_ref, g):
    return jnp.dot(hb, w_ref[:, g * ATT_WIDTH:(g + 1) * ATT_WIDTH], preferred_element_type=F32)


def _head_slabs(p, fill):
    lane = lax.broadcasted_iota(jnp.int32, (1, LANES), 1)
    low = lane < ATT_HEAD_DIM
    outs = []
    for j in range(ATT_WIDTH // LANES):
        xj = p[:, j * LANES:(j + 1) * LANES]
        outs.append(jnp.where(low, xj, fill))
        outs.append(jnp.where(low, pltpu.roll(xj, ATT_HEAD_DIM, 1), fill))
    return jnp.concatenate(outs, axis=-1)


def _in_prompt_kernel(x_ref, mod_ref, w_ref, ws_ref, cw_ref, alog_ref, dtb_ref, cos_ref, sin_ref,
                      qa_ref, k_ref, v_ref, ka_ref, vt_ref, km_ref, sza_ref,
                      qd_ref, kd_ref, vd_ref, szd_ref, gb_ref, nc_ref, xbuf, *, tm):
    t = pl.program_id(1)
    d = x_ref.shape[-1]
    hist = SUBLANES

    @pl.when(t == 0)
    def _():
        xbuf[0:hist, :] = jnp.zeros((hist, DN_QKV), F32)

    x = x_ref[0]
    shift = mod_ref[0][:, 0:d]
    scale = mod_ref[0][:, d:2 * d]
    h = x * (1.0 + scale) + shift
    hb = h.astype(BF16)
    cos = cos_ref[...]
    sin = sin_ref[...]

    lane = lax.broadcasted_iota(jnp.int32, (1, LANES), 1)
    qa_ref[0] = _head_slabs(_rope(_proj(hb, w_ref, 0), cos, sin) * QK_SCALE, 0.0)
    kr = _rope(_proj(hb, w_ref, 1), cos, sin)
    k_ref[0] = kr
    block_onehot = (lane == ATT_HEAD_DIM + t).astype(F32)
    ka_ref[0] = _head_slabs(kr, block_onehot).astype(BF16)
    km_ref[0, 0] = jnp.sum(kr, axis=0, keepdims=True) * (1.0 / tm)
    v = _proj(hb, w_ref, 2)
    v_ref[0] = v
    vt = v.T
    for hd in range(ATT_HEADS):
        vt_ref[0, 0, hd, 0:ATT_HEAD_DIM, :] = vt[hd * ATT_HEAD_DIM:(hd + 1) * ATT_HEAD_DIM, :].astype(BF16)
        vt_ref[0, 0, hd, ATT_HEAD_DIM:VT_ROWS, :] = jnp.ones((VT_ROWS - ATT_HEAD_DIM, tm), BF16)
    sza_ref[0] = _silu(_proj(hb, w_ref, 3))
    szd_ref[0] = _silu(_proj(hb, w_ref, 7))
    gb_ref[0] = _beta_decay(h, ws_ref, alog_ref, dtb_ref)

    outs = (qd_ref, kd_ref, vd_ref)
    for gi in range(3):
        cols = slice(gi * DN_WIDTH, (gi + 1) * DN_WIDTH)
        xbuf[hist:hist + tm, cols] = _proj(hb, w_ref, 4 + gi)
        conv = jnp.zeros((tm, DN_WIDTH), F32)
        for i in range(CONV_WIDTH):
            off = hist - (CONV_WIDTH - 1) + i
            conv = conv + xbuf[off:off + tm, cols] * cw_ref[i:i + 1, cols]
        c = _silu(conv)
        if gi == 0:
            c = _l2norm_heads(c, DN_HEAD_DIM ** -0.5)
        elif gi == 1:
            c = _l2norm_heads(c, 1.0)
        outs[gi][0] = c

    tail = xbuf[tm:tm + hist, :]
    nc_ref[0] = tail
    xbuf[0:hist, :] = tail


def _in_prompt_call(x, mod, w_main, w_small, conv_w, alog, dtb, cos, sin, tm):
    n, t, d = x.shape
    nt = t // tm
    row = lambda n_, t_: (n_, t_, 0)
    full = lambda n_, t_: (0, 0)
    wide = jax.ShapeDtypeStruct((n, t, ATT_WIDTH), F32)
    blk = pl.BlockSpec((1, tm, ATT_WIDTH), row)
    slab_w = ATT_HEADS * LANES
    slab_blk = pl.BlockSpec((1, tm, slab_w), row)
    return pl.pallas_call(
        functools.partial(_in_prompt_kernel, tm=tm),
        grid=(n, nt),
        in_specs=[
            pl.BlockSpec((1, tm, d), row),
            pl.BlockSpec((1, 1, 3 * d), lambda n_, t_: (n_, 0, 0)),
            pl.BlockSpec((d, MAIN_COLS), full),
            pl.BlockSpec((d, LANES), full),
            pl.BlockSpec((CONV_WIDTH, DN_QKV), full),
            pl.BlockSpec((1, LANES), full),
            pl.BlockSpec((1, LANES), full),
            pl.BlockSpec((tm, LANES), lambda n_, t_: (t_, 0)),
            pl.BlockSpec((tm, LANES), lambda n_, t_: (t_, 0)),
        ],
        out_specs=[
            slab_blk, blk, blk, slab_blk,
            pl.BlockSpec((1, 1, ATT_HEADS, VT_ROWS, tm), lambda n_, t_: (n_, t_, 0, 0, 0)),
            pl.BlockSpec((1, 1, 1, ATT_WIDTH), lambda n_, t_: (n_, t_, 0, 0)),
            blk, blk, blk, blk, blk,
            pl.BlockSpec((1, tm, LANES), row),
            pl.BlockSpec((1, SUBLANES, DN_QKV), lambda n_, t_: (n_, 0, 0)),
        ],
        out_shape=[
            jax.ShapeDtypeStruct((n, t, slab_w), F32), wide, wide,
            jax.ShapeDtypeStruct((n, t, slab_w), BF16),
            jax.ShapeDtypeStruct((n, nt, ATT_HEADS, VT_ROWS, tm), BF16),
            jax.ShapeDtypeStruct((n, nt, 1, ATT_WIDTH), F32),
            wide, wide, wide, wide, wide,
            jax.ShapeDtypeStruct((n, t, LANES), F32),
            jax.ShapeDtypeStruct((n, SUBLANES, DN_QKV), F32),
        ],
        scratch_shapes=[pltpu.VMEM((tm + SUBLANES, DN_QKV), F32)],
        compiler_params=_cparams(("arbitrary", "arbitrary")),
        name="in_prompt",
    )(x, mod, w_main, w_small, conv_w, alog, dtb, cos, sin)


def _in_sample_kernel(x_ref, shift_ref, scale_ref, w_ref, ws_ref, cw_ref, alog_ref, dtb_ref,
                      cos_ref, sin_ref, hist_ref,
                      q_ref, k_ref, v_ref, sza_ref, qd_ref, kd_ref, vd_ref, szd_ref, gb_ref, nc_ref,
                      xall, *, batch, steps):
    rows = batch * steps
    nh = (CONV_WIDTH - 1) * batch
    h = x_ref[...] * (1.0 + scale_ref[...]) + shift_ref[...]
    hb = h.astype(BF16)
    cos = cos_ref[...]
    sin = sin_ref[...]
    q_ref[...] = _rope(_proj(hb, w_ref, 0), cos, sin) * QK_SCALE
    k_ref[...] = _rope(_proj(hb, w_ref, 1), cos, sin)
    v_ref[...] = _proj(hb, w_ref, 2)
    sza_ref[...] = _silu(_proj(hb, w_ref, 3))
    szd_ref[...] = _silu(_proj(hb, w_ref, 7))
    gb_ref[...] = _beta_decay(h, ws_ref, alog_ref, dtb_ref)

    xall[0:nh, :] = hist_ref[...]
    outs = (qd_ref, kd_ref, vd_ref)
    for gi in range(3):
        cols = slice(gi * DN_WIDTH, (gi + 1) * DN_WIDTH)
        xall[nh:nh + rows, cols] = _proj(hb, w_ref, 4 + gi)
        conv = jnp.zeros((rows, DN_WIDTH), F32)
        for i in range(CONV_WIDTH):
            conv = conv + xall[i * batch:i * batch + rows, cols] * cw_ref[i:i + 1, cols]
        c = _silu(conv)
        if gi == 0:
            c = _l2norm_heads(c, DN_HEAD_DIM ** -0.5)
        elif gi == 1:
            c = _l2norm_heads(c, 1.0)
        outs[gi][...] = c
    nc_ref[...] = xall[rows:rows + nh, :]


def _in_sample_call(x_tm, shift, scale, w_main, w_small, conv_w, alog, dtb, cos, sin, hist_tm, batch, steps):
    rows = batch * steps
    nh = (CONV_WIDTH - 1) * batch
    wide = jax.ShapeDtypeStruct((rows, ATT_WIDTH), F32)
    return pl.pallas_call(
        functools.partial(_in_sample_kernel, batch=batch, steps=steps),
        out_shape=[wide] * 8 + [jax.ShapeDtypeStruct((rows, LANES), F32),
                                jax.ShapeDtypeStruct((nh, DN_QKV), F32)],
        scratch_shapes=[pltpu.VMEM((rows + nh, DN_QKV), F32)],
        compiler_params=pltpu.CompilerParams(vmem_limit_bytes=VMEM_LIMIT),
        name="in_sample",
    )(x_tm, shift, scale, w_main, w_small, conv_w, alog, dtb, cos, sin, hist_tm)


def _topk_bias(gate, valid, blkf, axis=-1):
    gate = jnp.where(valid, gate, -jnp.inf)
    bias = jnp.full(gate.shape, NEG, F32)
    for _ in range(MOBA_TOPK):
        mx = jnp.max(gate, axis=axis, keepdims=True)
        cand = (gate == mx) & (mx > -jnp.inf)
        idx = jnp.min(jnp.where(cand, blkf, 1e9), axis=axis, keepdims=True)
        pick = blkf == idx
        bias = jnp.where(pick, 0.0, bias)
        gate = jnp.where(pick, -jnp.inf, gate)
    return bias


def _moba_prompt_kernel(q_ref, k_ref, vt_ref, km_ref, o_ref, s_ref, m_ref, acc_ref):
    g = pl.program_id(2)
    bs = MOBA_BLOCK
    lane = lax.broadcasted_iota(jnp.int32, (1, LANES), 1)
    low = lane < ATT_HEAD_DIM
    brow = lax.broadcasted_iota(jnp.int32, (MAX_MOBA_BLOCKS, bs), 0)
    browf = brow.astype(F32)
    tk = TILE_BLOCKS * bs
    tl = (g * Q_BLOCKS) // TILE_BLOCKS
    krow = lax.broadcasted_iota(jnp.int32, (tk, bs), 0)
    qcol = lax.broadcasted_iota(jnp.int32, (tk, bs), 1)
    items = [(u, hh) for u in range(Q_BLOCKS) for hh in range(2)]

    def pv(tile, hh, p):
        pb = p.astype(BF16)
        out = None
        for b in range(TILE_BLOCKS):
            part = jnp.dot(vt_ref[0, tile * TILE_BLOCKS + b, hh], pb[b * bs:(b + 1) * bs],
                           preferred_element_type=F32)
            out = part if out is None else out + part
        return out

    qas = []
    for u, hh in items:
        i = g * Q_BLOCKS + u
        qs = q_ref[0, u * bs:(u + 1) * bs, hh * LANES:(hh + 1) * LANES]
        gate_t = _mm3(km_ref[0, hh, ATT_HEAD_DIM:, :], qs, NT_DIMS)
        bias = _topk_bias(gate_t, brow < i, browf, axis=0)
        bias = jnp.where(brow == i, 0.0, bias)
        bias_t = jnp.concatenate([jnp.full((ATT_HEAD_DIM, bs), NEG, F32), bias], axis=0)
        qas.append(jnp.where(low, qs, bias_t.T).astype(BF16))

    def scores(tile):
        off = pl.multiple_of(tile * tk, tk)
        return [lax.dot_general(k_ref[0, pl.ds(off, tk), hh * LANES:(hh + 1) * LANES], qas[it],
                                NT_DIMS, preferred_element_type=F32) for it, (u, hh) in enumerate(items)]

    for it, s in enumerate(scores(tl)):
        u, hh = items[it]
        causal = (krow + tl * tk) <= (qcol + (g * Q_BLOCKS + u) * bs)
        s = jnp.where(causal, s, NEG)
        m = jnp.max(s, axis=0, keepdims=True)
        m_ref[it] = m
        acc_ref[it] = pv(tl, hh, jnp.exp2(s - m))
    for it, s in enumerate(scores(jnp.maximum(tl - 1, 0))):
        s_ref[0, it] = s

    def body(t, carry):
        j = tl - 1 - t
        slot = t % 2
        cur = [s_ref[slot, it] for it in range(len(items))]
        nxt = scores(jnp.maximum(j - 1, 0))
        for it, (u, hh) in enumerate(items):
            m = m_ref[it]
            m_new = jnp.maximum(m, jnp.max(cur[it], axis=0, keepdims=True))
            acc_ref[it] = jnp.exp2(m - m_new) * acc_ref[it] + pv(j, hh, jnp.exp2(cur[it] - m_new))
            m_ref[it] = m_new
        for it in range(len(items)):
            s_ref[1 - slot, it] = nxt[it]
        return carry

    lax.fori_loop(0, tl, body, 0)
    d = ATT_HEAD_DIM
    for u in range(Q_BLOCKS):
        a0, a1 = acc_ref[2 * u], acc_ref[2 * u + 1]
        o = jnp.concatenate([a0[0:d] / a0[d:d + 1], a1[0:d] / a1[d:d + 1]], axis=0)
        o_ref[0, u * bs:(u + 1) * bs, :] = o.T


def _moba_prompt_call(q_aug, k_aug, vt, km_pad):
    n, t, _ = q_aug.shape
    nb = t // MOBA_BLOCK
    npair = ATT_WIDTH // LANES
    pair_w = 2 * LANES
    qrows = Q_BLOCKS * MOBA_BLOCK
    nitems = 2 * Q_BLOCKS
    return pl.pallas_call(
        _moba_prompt_kernel,
        grid=(n, npair, nb // Q_BLOCKS),
        in_specs=[
            pl.BlockSpec((1, qrows, pair_w), lambda n_, p_, i_: (n_, i_, p_)),
            pl.BlockSpec((1, t, pair_w), lambda n_, p_, i_: (n_, 0, p_)),
            pl.BlockSpec((1, nb, 2, VT_ROWS, MOBA_BLOCK), lambda n_, p_, i_: (n_, 0, p_, 0, 0)),
            pl.BlockSpec((1, 2, LANES, LANES), lambda n_, p_, i_: (n_, p_, 0, 0)),
        ],
        out_specs=pl.BlockSpec((1, qrows, LANES), lambda n_, p_, i_: (n_, i_, p_)),
        out_shape=jax.ShapeDtypeStruct((n, t, ATT_WIDTH), F32),
        scratch_shapes=[
            pltpu.VMEM((2, nitems, TILE_BLOCKS * MOBA_BLOCK, MOBA_BLOCK), F32),
            pltpu.VMEM((nitems, 1, MOBA_BLOCK), F32),
            pltpu.VMEM((nitems, VT_ROWS, MOBA_BLOCK), F32),
        ],
        compiler_params=_cparams(("arbitrary", "arbitrary", "arbitrary")),
        name="moba_prompt",
    )(q_aug, k_aug, vt, km_pad)


def _moba_sample_kernel(pt_ref, q_ref, kn_ref, vn_ref, *rest, npages, steps):
    pps = PAGES_PER_STEP
    kp = rest[:pps]
    vp = rest[pps:2 * pps]
    o_ref = rest[2 * pps]
    lg_ref, km_ref, acc_ref, lsum_ref = rest[2 * pps + 1:]
    s = pl.program_id(1)
    nsteps = npages // pps
    nblk = npages * PAGE_SIZE // MOBA_BLOCK
    ppb = MOBA_BLOCK // PAGE_SIZE
    hs = range(ATT_HEADS)
    half = TOK_ROWS // 2

    lane = lax.broadcasted_iota(jnp.int32, (1, LANES), 1)

    @pl.when(s == 0)
    def _():
        km_ref[...] = jnp.zeros(km_ref.shape, F32)

    @pl.when(s < nsteps)
    def _():
        qb = [q_ref[0, h].astype(BF16) for h in hs]
        for b in range(pps // ppb):
            page0 = s * pps + b * ppb
            for h in hs:
                kts = [kp[b * ppb + pg][h] for pg in range(ppb)]
                lg = jnp.dot(qb[h], jnp.concatenate(kts, axis=-1).astype(BF16),
                             preferred_element_type=F32)
                for pg in range(ppb):
                    lg_ref[page0 + pg, h] = lg[0:half, pg * PAGE_SIZE:(pg + 1) * PAGE_SIZE]
                ksum = kts[0]
                for pg in range(1, ppb):
                    ksum = ksum + kts[pg]
                col = jnp.sum(ksum, axis=-1, keepdims=True) * (1.0 / MOBA_BLOCK)
                km_ref[h] = jnp.where(lane == s * (pps // ppb) + b, col, km_ref[h])

    @pl.when(s == nsteps - 1)
    def _():
        rows = ATT_HEADS * half
        blk = lax.broadcasted_iota(jnp.int32, (rows, LANES), 1)
        trow = lax.broadcasted_iota(jnp.int32, (rows, TOK_ROWS), 0) % half
        tcol = lax.broadcasted_iota(jnp.int32, (rows, TOK_ROWS), 1)
        qf = [q_ref[0, h] for h in hs]
        gate = jnp.concatenate(
            [jnp.dot(qf[h][0:half], km_ref[h], precision=HIGHEST, preferred_element_type=F32)
             for h in hs], axis=0)
        bias = _topk_bias(gate, blk < nblk, blk.astype(F32))
        lo = jnp.concatenate(
            [lax.dot_general(qf[h].astype(BF16), kn_ref[0, h].astype(BF16), NT_DIMS,
                             preferred_element_type=F32)[0:half] for h in hs], axis=0)
        lo = jnp.where((tcol <= trow) & (tcol < steps), lo, NEG)
        bcols = [[bias[h * half:(h + 1) * half, b:b + 1] for b in range(nblk)] for h in hs]
        mts = []
        for h in hs:
            mt = jnp.full((half, PAGE_SIZE), NEG, F32)
            for b in range(nblk):
                for pg in range(ppb):
                    mt = jnp.maximum(mt, lg_ref[b * ppb + pg, h] + bcols[h][b])
            mts.append(mt)
        m = jnp.maximum(jnp.max(jnp.concatenate(mts, axis=0), axis=-1, keepdims=True),
                        jnp.max(lo, axis=-1, keepdims=True))
        po = jnp.exp2(lo - m)
        lts = []
        for h in hs:
            mh = m[h * half:(h + 1) * half]
            lt = jnp.zeros((half, PAGE_SIZE), F32)
            for b in range(nblk):
                for pg in range(ppb):
                    pb = jnp.exp2(lg_ref[b * ppb + pg, h] + bcols[h][b] - mh)
                    lg_ref[b * ppb + pg, h] = pb
                    lt = lt + pb
            lts.append(lt)
        lsum = (jnp.sum(jnp.concatenate(lts, axis=0), axis=-1, keepdims=True)
                + jnp.sum(po, axis=-1, keepdims=True))
        zeros = jnp.zeros((half, TOK_ROWS), F32)
        for h in hs:
            lsum_ref[h] = lsum[h * half:(h + 1) * half]
            po16 = jnp.concatenate([po[h * half:(h + 1) * half], zeros], axis=0).astype(BF16)
            acc_ref[h] = jnp.dot(po16, vn_ref[0, h].astype(BF16), preferred_element_type=F32)[0:half]

    @pl.when(s >= nsteps)
    def _():
        acc = [acc_ref[h] for h in hs]
        zeros = jnp.zeros((half, MOBA_BLOCK), F32)
        for b in range(pps // ppb):
            page0 = (s - nsteps) * pps + b * ppb
            for h in hs:
                pblk = jnp.concatenate([lg_ref[page0 + pg, h] for pg in range(ppb)], axis=-1)
                p16 = jnp.concatenate([pblk, zeros], axis=0).astype(BF16)
                vt = jnp.concatenate([vp[b * ppb + pg][h] for pg in range(ppb)], axis=-1).astype(BF16)
                acc[h] = acc[h] + lax.dot_general(p16, vt, NT_DIMS, preferred_element_type=F32)[0:half]
        for h in hs:
            acc_ref[h] = acc[h]

    @pl.when(s == 2 * nsteps - 1)
    def _():
        for h in hs:
            o_ref[0, h] = acc_ref[h] / lsum_ref[h]


def _moba_sample_call(page_flat, qh, knh, vnh, cache_k4, cache_v4, layer, npages, steps):
    batch = qh.shape[0]
    pps = PAGES_PER_STEP
    nsteps = npages // pps
    half = TOK_ROWS // 2
    page_block = (None, None, ATT_HEADS, ATT_HEAD_DIM, PAGE_SIZE)
    tok = pl.BlockSpec((1, ATT_HEADS, TOK_ROWS, ATT_HEAD_DIM), lambda n_, s_, pt: (n_, 0, 0, 0))

    def kspec(i):
        return pl.BlockSpec(
            page_block,
            lambda n_, s_, pt: (layer, pt[n_ * npages + jnp.minimum(s_, nsteps - 1) * pps + i], 0, 0, 0))

    def vspec(i):
        return pl.BlockSpec(
            page_block,
            lambda n_, s_, pt: (layer, pt[n_ * npages + jnp.maximum(s_ - nsteps, 0) * pps + i], 0, 0, 0))

    grid_spec = pltpu.PrefetchScalarGridSpec(
        num_scalar_prefetch=1,
        grid=(batch, 2 * nsteps),
        in_specs=[tok, tok, tok] + [kspec(i) for i in range(pps)] + [vspec(i) for i in range(pps)],
        out_specs=pl.BlockSpec((1, ATT_HEADS, half, ATT_HEAD_DIM), lambda n_, s_, pt: (n_, 0, 0, 0)),
        scratch_shapes=[
            pltpu.VMEM((npages, ATT_HEADS, half, PAGE_SIZE), F32),
            pltpu.VMEM((ATT_HEADS, ATT_HEAD_DIM, LANES), F32),
            pltpu.VMEM((ATT_HEADS, half, ATT_HEAD_DIM), F32),
            pltpu.VMEM((ATT_HEADS, half, 1), F32),
        ],
    )
    return pl.pallas_call(
        functools.partial(_moba_sample_kernel, npages=npages, steps=steps),
        grid_spec=grid_spec,
        out_shape=jax.ShapeDtypeStruct((batch, ATT_HEADS, half, ATT_HEAD_DIM), F32),
        compiler_params=_cparams(("arbitrary", "arbitrary")),
        name="moba_sample",
    )(page_flat, qh, knh, vnh, *([cache_k4] * pps), *([cache_v4] * pps))


def _gdn_kernel(q_ref, k_ref, v_ref, gb_ref, s0_ref, o_ref, sout_ref, s_scr, *, chunk, cps):
    c = pl.program_id(1)
    nc = pl.num_programs(1)

    @pl.when(c == 0)
    def _():
        s_scr[...] = s0_ref[0]

    ri = lax.broadcasted_iota(jnp.int32, (chunk, chunk), 0)
    ci = lax.broadcasted_iota(jnp.int32, (chunk, chunk), 1)
    tril = ri >= ci
    stril = ri > ci
    eye = (ri == ci).astype(F32)
    nsq = int(round(math.log2(chunk))) - 1

    def mm1(a, b, dims=NN_DIMS):
        return lax.dot_general(a.astype(BF16), b.astype(BF16), dims, preferred_element_type=F32)

    hs = range(DN_HEADS)
    cols = [slice(h * DN_HEAD_DIM, (h + 1) * DN_HEAD_DIM) for h in hs]
    items = [(cc, h) for cc in range(cps) for h in hs]
    rows = [slice(cc * chunk, (cc + 1) * chunk) for cc in range(cps)]
    gb = [gb_ref[0, rows[cc], :] for cc in range(cps)]
    gcum = [jnp.dot(tril.astype(F32), gb[cc], precision=HIGHEST, preferred_element_type=F32)
            for cc in range(cps)]
    gcum_rows = [lax.dot_general(gcum[cc], eye, TN_DIMS, precision=HIGHEST, preferred_element_type=F32)
                 for cc in range(cps)]
    q = {it: q_ref[0, rows[it[0]], cols[it[1]]] for it in items}
    k = {it: k_ref[0, rows[it[0]], cols[it[1]]] for it in items}
    v = {it: v_ref[0, rows[it[0]], cols[it[1]]] for it in items}
    beta = {(cc, h): gb[cc][:, h:h + 1] for cc, h in items}
    gc = {(cc, h): gcum[cc][:, DN_HEADS + h:DN_HEADS + h + 1] for cc, h in items}
    kbeta = {it: k[it] * beta[it] for it in items}
    kk = {it: _mm3(kbeta[it], k[it], NT_DIMS) for it in items}
    qk = {it: mm1(q[it], k[it], NT_DIMS) for it in items}
    decay = {}
    for cc, h in items:
        diff = gc[cc, h] - gcum_rows[cc][DN_HEADS + h:DN_HEADS + h + 1, :]
        decay[cc, h] = jnp.where(tril, jnp.exp(jnp.where(tril, diff, 0.0)), 0.0)
    eg = {it: jnp.exp(gc[it]) for it in items}
    pw = {it: -jnp.where(stril, kk[it] * decay[it], 0.0) for it in items}
    y = {it: jnp.concatenate([v[it] * beta[it], kbeta[it] * eg[it]], axis=-1) for it in items}
    y = {it: y[it] + _mm3(pw[it], y[it]) for it in items}
    for _ in range(nsq):
        pw = {it: _mm3(pw[it], pw[it]) for it in items}
        y = {it: y[it] + _mm3(pw[it], y[it]) for it in items}

    state = [s_scr[h] for h in hs]
    for cc in range(cps):
        its = [(cc, h) for h in hs]
        v_new = [y[it][:, 0:DN_HEAD_DIM] - _mm3(y[it][:, DN_HEAD_DIM:2 * DN_HEAD_DIM], state[it[1]])
                 for it in its]
        g_last = [gc[it][chunk - 1:chunk, :] for it in its]
        kv = [_mm3(k[it] * jnp.exp(g_last[h] - gc[it]), v_new[h], TN_DIMS) for h, it in enumerate(its)]
        for h, it in enumerate(its):
            attn = jnp.where(tril, qk[it] * decay[it], 0.0)
            o_ref[0, rows[cc], cols[h]] = mm1(q[it] * eg[it], state[h]) + mm1(attn, v_new[h])
        state = [state[h] * jnp.exp(g_last[h]) + kv[h] for h in hs]
    for h in hs:
        s_scr[h] = state[h]

    @pl.when(c == nc - 1)
    def _():
        sout_ref[0] = s_scr[...]


def _gdn_call(qd, kd, vd, gb, s0, s0_index, chunk, cps):
    n, t, _ = qd.shape
    step_rows = chunk * cps
    nc = t // step_rows
    row = lambda n_, c_: (n_, c_, 0)
    blk = pl.BlockSpec((1, step_rows, DN_WIDTH), row)
    sshape = (DN_HEADS, DN_HEAD_DIM, DN_HEAD_DIM)
    s0_block = (None,) * (s0.ndim - 4) + (1,) + sshape
    return pl.pallas_call(
        functools.partial(_gdn_kernel, chunk=chunk, cps=cps),
        grid=(n, nc),
        in_specs=[blk, blk, blk,
                  pl.BlockSpec((1, step_rows, LANES), row),
                  pl.BlockSpec(s0_block, lambda n_, c_: s0_index(n_))],
        out_specs=[blk, pl.BlockSpec((1,) + sshape, lambda n_, c_: (n_, 0, 0, 0))],
        out_shape=[jax.ShapeDtypeStruct((n, t, DN_WIDTH), F32),
                   jax.ShapeDtypeStruct((n,) + sshape, F32)],
        scratch_shapes=[pltpu.VMEM(sshape, F32)],
        compiler_params=_cparams(("arbitrary", "arbitrary")),
        name="gdn",
    )(qd, kd, vd, gb, s0)


def _out_kernel(oa_ref, sza_ref, od_ref, szd_ref, x_ref, gate_ref, wo_ref, nw_ref, lg_ref, lb_ref,
                o_ref, *, alpha):
    ya = (oa_ref[0] * sza_ref[0]).astype(BF16)
    od = od_ref[0]
    parts = []
    for h in range(DN_HEADS):
        oh = od[:, h * DN_HEAD_DIM:(h + 1) * DN_HEAD_DIM]
        ms = jnp.mean(oh * oh, axis=-1, keepdims=True)
        parts.append(oh * lax.rsqrt(ms + RMS_EPS) * nw_ref[...])
    yd = (jnp.concatenate(parts, axis=-1) * szd_ref[0]).astype(BF16)
    y = (jnp.dot(ya, wo_ref[0:ATT_WIDTH, :], preferred_element_type=F32)
         + jnp.dot(yd, wo_ref[ATT_WIDTH:ATT_WIDTH + DN_WIDTH, :], preferred_element_type=F32))
    r = alpha * x_ref[0] + (1.0 + gate_ref[0]) * y
    mu = jnp.mean(r, axis=-1, keepdims=True)
    rc = r - mu
    var = jnp.mean(rc * rc, axis=-1, keepdims=True)
    o_ref[0] = rc * lax.rsqrt(var + LN_EPS) * lg_ref[...] + lb_ref[...]


def _out_call(oa, sza, od, szd, x, gate, w_out_bf, nw, lg, lb, tm, alpha):
    n, t, d = x.shape
    nt = t // tm
    row = lambda n_, t_: (n_, t_, 0)
    full = lambda n_, t_: (0, 0)
    blk = pl.BlockSpec((1, tm, ATT_WIDTH), row)
    if gate.shape[1] == 1:
        gate_spec = pl.BlockSpec((1, 1, d), lambda n_, t_: (n_, 0, 0))
    else:
        gate_spec = pl.BlockSpec((1, tm, d), row)
    return pl.pallas_call(
        functools.partial(_out_kernel, alpha=alpha),
        grid=(n, nt),
        in_specs=[blk, blk, blk, blk,
                  pl.BlockSpec((1, tm, d), row),
                  gate_spec,
                  pl.BlockSpec((ATT_WIDTH + DN_WIDTH, d), full),
                  pl.BlockSpec((1, DN_HEAD_DIM), full),
                  pl.BlockSpec((1, d), full),
                  pl.BlockSpec((1, d), full)],
        out_specs=pl.BlockSpec((1, tm, d), row),
        out_shape=jax.ShapeDtypeStruct((n, t, d), F32),
        compiler_params=_cparams(("arbitrary", "arbitrary")),
        name="out_proj",
    )(oa, sza, od, szd, x, gate, w_out_bf, nw, lg, lb)


def _rope_tables(pos):
    half = ATT_HEAD_DIM // 2
    inv_freq = ROPE_THETA ** (-jnp.arange(half, dtype=F32) / half)
    ang = pos.astype(F32)[:, None] * inv_freq[None, :]
    cos = jnp.cos(ang)
    sin = jnp.sin(ang)
    reps = LANES // ATT_HEAD_DIM
    return jnp.tile(cos, (1, 2 * reps)), jnp.tile(jnp.concatenate([-sin, sin], axis=1), (1, reps))


def _lane_row(vec, offset):
    return jnp.zeros((1, LANES), F32).at[0, offset:offset + vec.shape[0]].set(vec.astype(F32))


def kernel(x_prompt, x_sample, c_prompt, c_sample, cache_k, cache_v, page_table, state_ssm, state_conv,
           w_ada, b_ada, w_in, conv_w, a_log, dt_bias, dn_norm_w, w_out, ln_g, ln_b):
    depth, d, _ = w_ada.shape
    bp, seq, _ = x_prompt.shape
    bs, dec, _ = x_sample.shape
    npages = page_table.shape[1]
    past = npages * PAGE_SIZE
    n_pool = cache_k.shape[1]
    alpha = float((2 * depth) ** 0.25)
    tm = MOBA_BLOCK
    nb = seq // tm
    assert seq % tm == 0 and past % MOBA_BLOCK == 0 and nb <= MAX_MOBA_BLOCKS and nb % TILE_BLOCKS == 0
    assert seq % (DN_CHUNK * GDN_CHUNKS_PER_STEP) == 0
    assert dec <= SUBLANES and npages % PAGES_PER_STEP == 0

    nc_rows = bp + bs
    c_rows = -(-nc_rows // SUBLANES) * SUBLANES
    c_all = jnp.zeros((c_rows, d), F32).at[:bp].set(c_prompt).at[bp:nc_rows].set(c_sample)
    mod_all = _mod_call(c_all, w_ada, b_ada)

    cos_p, sin_p = _rope_tables(jnp.arange(seq, dtype=jnp.int32))
    pos_s = past + jnp.repeat(jnp.arange(dec, dtype=jnp.int32), bs)
    cos_s, sin_s = _rope_tables(pos_s)

    cache_kt = cache_k.transpose(0, 1, 3, 4, 2)
    cache_vt = cache_v.transpose(0, 1, 3, 4, 2)
    page_flat = page_table.reshape(-1).astype(jnp.int32)
    zero_state = jnp.zeros((bp, DN_HEADS, DN_HEAD_DIM, DN_HEAD_DIM), F32)

    def to_bm8(a):
        w = a.shape[-1]
        a = a.reshape(dec, bs, w).transpose(1, 0, 2)
        return jnp.pad(a, ((0, 0), (0, SUBLANES - dec), (0, 0)))

    def to_heads(a):
        a = a.reshape(dec, bs, ATT_HEADS, ATT_HEAD_DIM).transpose(1, 2, 0, 3)
        return jnp.pad(a, ((0, 0), (0, 0), (0, TOK_ROWS - dec), (0, 0)))

    xp = x_prompt
    xs_tm = x_sample.transpose(1, 0, 2).reshape(dec * bs, d)
    outs = [[] for _ in range(8)]
    for l in range(depth):
        w_main = w_in[l, :, :MAIN_COLS].astype(BF16)
        w_small = jnp.zeros((d, LANES), F32).at[:, :2 * DN_HEADS].set(w_in[l, :, MAIN_COLS:])
        alog = _lane_row(a_log[l], DN_HEADS)
        dtb = _lane_row(dt_bias[l], DN_HEADS)
        w_out_bf = w_out[l].astype(BF16)
        nw = dn_norm_w[l].reshape(1, DN_HEAD_DIM)
        lg = ln_g[l].reshape(1, d)
        lb = ln_b[l].reshape(1, d)
        mod_p = mod_all[l, :bp].reshape(bp, 1, 3 * d)
        mod_s = mod_all[l, bp:nc_rows]

        (q_aug, k, v, k_aug, vt, kmean, sza, qd, kd, vd, szd, gb, nconv) = _in_prompt_call(
            xp, mod_p, w_main, w_small, conv_w[l], alog, dtb, cos_p, sin_p, tm)
        km_heads = kmean.reshape(bp, nb, ATT_HEADS, ATT_HEAD_DIM).transpose(0, 2, 1, 3)
        km_pad = jnp.pad(km_heads, ((0, 0), (0, 0), (ATT_HEAD_DIM, LANES - ATT_HEAD_DIM - nb),
                                    (0, LANES - ATT_HEAD_DIM)))
        oa = _moba_prompt_call(q_aug, k_aug, vt, km_pad)
        od, s_new = _gdn_call(qd, kd, vd, gb, zero_state, lambda n_: (n_, 0, 0, 0), DN_CHUNK,
                              GDN_CHUNKS_PER_STEP)
        xp = _out_call(oa, sza, od, szd, xp, mod_p[:, :, 2 * d:], w_out_bf, nw, lg, lb, tm, alpha)
        outs[0].append(k.reshape(bp, seq, ATT_HEADS, ATT_HEAD_DIM))
        outs[1].append(v.reshape(bp, seq, ATT_HEADS, ATT_HEAD_DIM))
        outs[2].append(s_new)
        outs[3].append(nconv[:, SUBLANES - (CONV_WIDTH - 1):, :])

        mod_rows = jnp.tile(mod_s, (dec, 1))
        hist_tm = state_conv[l].transpose(1, 0, 2).reshape((CONV_WIDTH - 1) * bs, DN_QKV)
        (q, k, v, sza, qd, kd, vd, szd, gb, nconv) = _in_sample_call(
            xs_tm, mod_rows[:, :d], mod_rows[:, d:2 * d], w_main, w_small, conv_w[l], alog, dtb,
            cos_s, sin_s, hist_tm, bs, dec)
        k8, v8 = to_bm8(k), to_bm8(v)
        oh = _moba_sample_call(page_flat, to_heads(q), to_heads(k), to_heads(v), cache_kt, cache_vt,
                               l, npages, dec)
        oa8 = oh.transpose(0, 2, 1, 3).reshape(bs, SUBLANES, ATT_WIDTH)
        od8, s_new = _gdn_call(to_bm8(qd), to_bm8(kd), to_bm8(vd), to_bm8(gb), state_ssm,
                               lambda n_, l=l: (l, n_, 0, 0, 0), SUBLANES, 1)
        rows8 = bs * SUBLANES
        flat = lambda a: a.reshape(1, rows8, a.shape[-1])
        gate_rows = jnp.repeat(mod_s[:, 2 * d:], SUBLANES, axis=0).reshape(1, rows8, d)
        xs8 = _out_call(flat(oa8), flat(to_bm8(sza)), flat(od8), flat(to_bm8(szd)), flat(to_bm8(xs_tm)),
                        gate_rows, w_out_bf, nw, lg, lb, rows8, alpha)
        xs_bm = xs8.reshape(bs, SUBLANES, d)[:, :dec]
        xs_tm = xs_bm.transpose(1, 0, 2).reshape(dec * bs, d)
        outs[4].append(k8[:, :dec].reshape(bs, dec, ATT_HEADS, ATT_HEAD_DIM))
        outs[5].append(v8[:, :dec].reshape(bs, dec, ATT_HEADS, ATT_HEAD_DIM))
        outs[6].append(s_new)
        outs[7].append(nconv.reshape(CONV_WIDTH - 1, bs, DN_QKV).transpose(1, 0, 2))

    stacked = [jnp.stack(o) for o in outs]
    return (xp, xs_bm, stacked[0], stacked[1], stacked[2], stacked[3],
            stacked[4], stacked[5], stacked[6], stacked[7])
```

```python
import functools
import math

import jax
import jax.numpy as jnp
from jax import lax
from jax.experimental import pallas as pl
from jax.experimental.pallas import tpu as pltpu

F32 = jnp.float32
BF16 = jnp.bfloat16
HIGHEST = lax.Precision.HIGHEST

ATT_HEADS = 8
ATT_HEAD_DIM = 64
ATT_WIDTH = ATT_HEADS * ATT_HEAD_DIM
DN_HEADS = 4
DN_HEAD_DIM = 128
DN_WIDTH = DN_HEADS * DN_HEAD_DIM
DN_QKV = 3 * DN_WIDTH
MAIN_COLS = 4 * ATT_WIDTH + DN_QKV + DN_WIDTH
MOBA_BLOCK = 256
MOBA_TOPK = 3
DN_CHUNK = 64
GDN_CHUNKS_PER_STEP = 4
CONV_WIDTH = 4
PAGE_SIZE = 128
ROPE_THETA = 10000.0
LN_EPS = 1e-5
RMS_EPS = 1e-6
L2_EPS = 1e-6
ATT_SCALE = ATT_HEAD_DIM ** -0.5
QK_SCALE = ATT_SCALE * math.log2(math.e)
NEG = -1e30
LANES = 128
SUBLANES = 8
TOK_ROWS = 2 * SUBLANES
VT_ROWS = ATT_HEAD_DIM + 16
MAX_MOBA_BLOCKS = LANES - ATT_HEAD_DIM
TILE_BLOCKS = 2
Q_BLOCKS = 1
assert TILE_BLOCKS % Q_BLOCKS == 0
PAGES_PER_STEP = 16
VMEM_LIMIT = 56 * 1024 * 1024

NN_DIMS = (((1,), (0,)), ((), ()))
NT_DIMS = (((1,), (1,)), ((), ()))
TN_DIMS = (((0,), (0,)), ((), ()))


def _sigmoid(x):
    return 1.0 / (1.0 + jnp.exp(-x))


def _silu(x):
    return x * _sigmoid(x)


def _softplus(x):
    return jnp.maximum(x, 0.0) + jnp.log1p(jnp.exp(-jnp.abs(x)))


def _split_bf16(a):
    hi = a.astype(BF16)
    return hi, (a - hi.astype(F32)).astype(BF16)


def _mm3(a, b, dims=NN_DIMS):
    ah, al = _split_bf16(a)
    bh, bl = _split_bf16(b)

    def dot(x, y):
        return lax.dot_general(x, y, dims, preferred_element_type=F32)

    return (dot(ah, bl) + dot(al, bh)) + dot(ah, bh)


def _cparams(sem):
    return pltpu.CompilerParams(dimension_semantics=sem, vmem_limit_bytes=VMEM_LIMIT)


def _mod_kernel(c_ref, w_ref, b_ref, o_ref):
    s = _silu(c_ref[...])
    o_ref[0] = jnp.dot(s.astype(BF16), w_ref[0].astype(BF16), preferred_element_type=F32) + b_ref[0]


def _mod_call(c_all, w_ada, b_ada):
    depth, d, d3 = w_ada.shape
    rows = c_all.shape[0]
    nj = d3 // d
    return pl.pallas_call(
        _mod_kernel,
        grid=(depth, nj),
        in_specs=[
            pl.BlockSpec((rows, d), lambda l, j: (0, 0)),
            pl.BlockSpec((1, d, d), lambda l, j: (l, 0, j)),
            pl.BlockSpec((1, 1, d), lambda l, j: (l, 0, j)),
        ],
        out_specs=pl.BlockSpec((1, rows, d), lambda l, j: (l, 0, j)),
        out_shape=jax.ShapeDtypeStruct((depth, rows, d3), F32),
        compiler_params=_cparams(("arbitrary", "arbitrary")),
        name="mod",
    )(c_all, w_ada, b_ada.reshape(depth, 1, d3))


def _rope(p, cos, sin_signed):
    lane = lax.broadcasted_iota(jnp.int32, (1, LANES), 1)
    first_half = (lane % ATT_HEAD_DIM) < (ATT_HEAD_DIM // 2)
    outs = []
    for j in range(ATT_WIDTH // LANES):
        xj = p[:, j * LANES:(j + 1) * LANES]
        partner = jnp.where(first_half, pltpu.roll(xj, LANES - 32, 1), pltpu.roll(xj, 32, 1))
        outs.append(xj * cos + partner * sin_signed)
    return jnp.concatenate(outs, axis=-1)


def _l2norm_heads(c, scale):
    outs = []
    for h in range(DN_HEADS):
        ch = c[:, h * DN_HEAD_DIM:(h + 1) * DN_HEAD_DIM]
        ss = jnp.sum(ch * ch, axis=-1, keepdims=True)
        outs.append(ch * (lax.rsqrt(ss + L2_EPS) * scale))
    return jnp.concatenate(outs, axis=-1)


def _beta_decay(h, ws_ref, alog_ref, dtb_ref):
    raw = _mm3(h, ws_ref[...])
    lane = lax.broadcasted_iota(jnp.int32, (1, LANES), 1)
    beta = _sigmoid(raw)
    g = -jnp.exp(alog_ref[...]) * _softplus(raw + dtb_ref[...])
    return jnp.where(lane < DN_HEADS, beta, jnp.where(lane < 2 * DN_HEADS, g, 0.0))


def _proj(hb, w_ref, g):
    return jnp.dot(hb, w_ref[:, g * ATT_WIDTH:(g + 1) * ATT_WIDTH], preferred_element_type=F32)


def _head_slabs(p, fill):
    lane = lax.broadcasted_iota(jnp.int32, (1, LANES), 1)
    low = lane < ATT_HEAD_DIM
    outs = []
    for j in range(ATT_WIDTH // LANES):
        xj = p[:, j * LANES:(j + 1) * LANES]
        outs.append(jnp.where(low, xj, fill))
        outs.append(jnp.where(low, pltpu.roll(xj, ATT_HEAD_DIM, 1), fill))
    return jnp.concatenate(outs, axis=-1)


def _in_prompt_kernel(x_ref, mod_ref, w_ref, ws_ref, cw_ref, alog_ref, dtb_ref, cos_ref, sin_ref,
                      qa_ref, k_ref, v_ref, ka_ref, vt_ref, km_ref, sza_ref,
                      qd_ref, kd_ref, vd_ref, szd_ref, gb_ref, nc_ref, xbuf, *, tm):
    t = pl.program_id(1)
    d = x_ref.shape[-1]
    hist = SUBLANES

    @pl.when(t == 0)
    def _():
        xbuf[0:hist, :] = jnp.zeros((hist, DN_QKV), F32)

    x = x_ref[0]
    shift = mod_ref[0][:, 0:d]
    scale = mod_ref[0][:, d:2 * d]
    h = x * (1.0 + scale) + shift
    hb = h.astype(BF16)
    cos = cos_ref[...]
    sin = sin_ref[...]

    lane = lax.broadcasted_iota(jnp.int32, (1, LANES), 1)
    qa_ref[0] = _head_slabs(_rope(_proj(hb, w_ref, 0), cos, sin) * QK_SCALE, 0.0)
    kr = _rope(_proj(hb, w_ref, 1), cos, sin)
    k_ref[0] = kr
    block_onehot = (lane == ATT_HEAD_DIM + t).astype(F32)
    ka_ref[0] = _head_slabs(kr, block_onehot).astype(BF16)
    km_ref[0, 0] = jnp.sum(kr, axis=0, keepdims=True) * (1.0 / tm)
    v = _proj(hb, w_ref, 2)
    v_ref[0] = v
    vt = v.T
    for hd in range(ATT_HEADS):
        vt_ref[0, 0, hd, 0:ATT_HEAD_DIM, :] = vt[hd * ATT_HEAD_DIM:(hd + 1) * ATT_HEAD_DIM, :].astype(BF16)
        vt_ref[0, 0, hd, ATT_HEAD_DIM:VT_ROWS, :] = jnp.ones((VT_ROWS - ATT_HEAD_DIM, tm), BF16)
    sza_ref[0] = _silu(_proj(hb, w_ref, 3))
    szd_ref[0] = _silu(_proj(hb, w_ref, 7))
    gb_ref[0] = _beta_decay(h, ws_ref, alog_ref, dtb_ref)

    outs = (qd_ref, kd_ref, vd_ref)
    for gi in range(3):
        cols = slice(gi * DN_WIDTH, (gi + 1) * DN_WIDTH)
        xbuf[hist:hist + tm, cols] = _proj(hb, w_ref, 4 + gi)
        conv = jnp.zeros((tm, DN_WIDTH), F32)
        for i in range(CONV_WIDTH):
            off = hist - (CONV_WIDTH - 1) + i
            conv = conv + xbuf[off:off + tm, cols] * cw_ref[i:i + 1, cols]
        c = _silu(conv)
        if gi == 0:
            c = _l2norm_heads(c, DN_HEAD_DIM ** -0.5)
        elif gi == 1:
            c = _l2norm_heads(c, 1.0)
        outs[gi][0] = c

    tail = xbuf[tm:tm + hist, :]
    nc_ref[0] = tail
    xbuf[0:hist, :] = tail


def _in_prompt_call(x, mod, w_main, w_small, conv_w, alog, dtb, cos, sin, tm):
    n, t, d = x.shape
    nt = t // tm
    row = lambda n_, t_: (n_, t_, 0)
    full = lambda n_, t_: (0, 0)
    wide = jax.ShapeDtypeStruct((n, t, ATT_WIDTH), F32)
    blk = pl.BlockSpec((1, tm, ATT_WIDTH), row)
    slab_w = ATT_HEADS * LANES
    slab_blk = pl.BlockSpec((1, tm, slab_w), row)
    return pl.pallas_call(
        functools.partial(_in_prompt_kernel, tm=tm),
        grid=(n, nt),
        in_specs=[
            pl.BlockSpec((1, tm, d), row),
            pl.BlockSpec((1, 1, 3 * d), lambda n_, t_: (n_, 0, 0)),
            pl.BlockSpec((d, MAIN_COLS), full),
            pl.BlockSpec((d, LANES), full),
            pl.BlockSpec((CONV_WIDTH, DN_QKV), full),
            pl.BlockSpec((1, LANES), full),
            pl.BlockSpec((1, LANES), full),
            pl.BlockSpec((tm, LANES), lambda n_, t_: (t_, 0)),
            pl.BlockSpec((tm, LANES), lambda n_, t_: (t_, 0)),
        ],
        out_specs=[
            slab_blk, blk, blk, slab_blk,
            pl.BlockSpec((1, 1, ATT_HEADS, VT_ROWS, tm), lambda n_, t_: (n_, t_, 0, 0, 0)),
            pl.BlockSpec((1, 1, 1, ATT_WIDTH), lambda n_, t_: (n_, t_, 0, 0)),
            blk, blk, blk, blk, blk,
            pl.BlockSpec((1, tm, LANES), row),
            pl.BlockSpec((1, SUBLANES, DN_QKV), lambda n_, t_: (n_, 0, 0)),
        ],
        out_shape=[
            jax.ShapeDtypeStruct((n, t, slab_w), F32), wide, wide,
            jax.ShapeDtypeStruct((n, t, slab_w), BF16),
            jax.ShapeDtypeStruct((n, nt, ATT_HEADS, VT_ROWS, tm), BF16),
            jax.ShapeDtypeStruct((n, nt, 1, ATT_WIDTH), F32),
            wide, wide, wide, wide, wide,
            jax.ShapeDtypeStruct((n, t, LANES), F32),
            jax.ShapeDtypeStruct((n, SUBLANES, DN_QKV), F32),
        ],
        scratch_shapes=[pltpu.VMEM((tm + SUBLANES, DN_QKV), F32)],
        compiler_params=_cparams(("arbitrary", "arbitrary")),
        name="in_prompt",
    )(x, mod, w_main, w_small, conv_w, alog, dtb, cos, sin)


def _in_sample_kernel(x_ref, shift_ref, scale_ref, w_ref, ws_ref, cw_ref, alog_ref, dtb_ref,
                      cos_ref, sin_ref, hist_ref,
                      q_ref, k_ref, v_ref, sza_ref, qd_ref, kd_ref, vd_ref, szd_ref, gb_ref, nc_ref,
                      xall, *, batch, steps):
    rows = batch * steps
    nh = (CONV_WIDTH - 1) * batch
    h = x_ref[...] * (1.0 + scale_ref[...]) + shift_ref[...]
    hb = h.astype(BF16)
    cos = cos_ref[...]
    sin = sin_ref[...]
    q_ref[...] = _rope(_proj(hb, w_ref, 0), cos, sin) * QK_SCALE
    k_ref[...] = _rope(_proj(hb, w_ref, 1), cos, sin)
    v_ref[...] = _proj(hb, w_ref, 2)
    sza_ref[...] = _silu(_proj(hb, w_ref, 3))
    szd_ref[...] = _silu(_proj(hb, w_ref, 7))
    gb_ref[...] = _beta_decay(h, ws_ref, alog_ref, dtb_ref)

    xall[0:nh, :] = hist_ref[...]
    outs = (qd_ref, kd_ref, vd_ref)
    for gi in range(3):
        cols = slice(gi * DN_WIDTH, (gi + 1) * DN_WIDTH)
        xall[nh:nh + rows, cols] = _proj(hb, w_ref, 4 + gi)
        conv = jnp.zeros((rows, DN_WIDTH), F32)
        for i in range(CONV_WIDTH):
            conv = conv + xall[i * batch:i * batch + rows, cols] * cw_ref[i:i + 1, cols]
        c = _silu(conv)
        if gi == 0:
            c = _l2norm_heads(c, DN_HEAD_DIM ** -0.5)
        elif gi == 1:
            c = _l2norm_heads(c, 1.0)
        outs[gi][...] = c
    nc_ref[...] = xall[rows:rows + nh, :]


def _in_sample_call(x_tm, shift, scale, w_main, w_small, conv_w, alog, dtb, cos, sin, hist_tm, batch, steps):
    rows = batch * steps
    nh = (CONV_WIDTH - 1) * batch
    wide = jax.ShapeDtypeStruct((rows, ATT_WIDTH), F32)
    return pl.pallas_call(
        functools.partial(_in_sample_kernel, batch=batch, steps=steps),
        out_shape=[wide] * 8 + [jax.ShapeDtypeStruct((rows, LANES), F32),
                                jax.ShapeDtypeStruct((nh, DN_QKV), F32)],
        scratch_shapes=[pltpu.VMEM((rows + nh, DN_QKV), F32)],
        compiler_params=pltpu.CompilerParams(vmem_limit_bytes=VMEM_LIMIT),
        name="in_sample",
    )(x_tm, shift, scale, w_main, w_small, conv_w, alog, dtb, cos, sin, hist_tm)


def _topk_bias(gate, valid, blkf, axis=-1):
    gate = jnp.where(valid, gate, -jnp.inf)
    bias = jnp.full(gate.shape, NEG, F32)
    for _ in range(MOBA_TOPK):
        mx = jnp.max(gate, axis=axis, keepdims=True)
        cand = (gate == mx) & (mx > -jnp.inf)
        idx = jnp.min(jnp.where(cand, blkf, 1e9), axis=axis, keepdims=True)
        pick = blkf == idx
        bias = jnp.where(pick, 0.0, bias)
        gate = jnp.where(pick, -jnp.inf, gate)
    return bias


def _moba_prompt_kernel(q_ref, k_ref, vt_ref, km_ref, o_ref, s_ref, m_ref, acc_ref):
    g = pl.program_id(2)
    bs = MOBA_BLOCK
    lane = lax.broadcasted_iota(jnp.int32, (1, LANES), 1)
    low = lane < ATT_HEAD_DIM
    brow = lax.broadcasted_iota(jnp.int32, (MAX_MOBA_BLOCKS, bs), 0)
    browf = brow.astype(F32)
    tk = TILE_BLOCKS * bs
    tl = (g * Q_BLOCKS) // TILE_BLOCKS
    krow = lax.broadcasted_iota(jnp.int32, (tk, bs), 0)
    qcol = lax.broadcasted_iota(jnp.int32, (tk, bs), 1)
    items = [(u, hh) for u in range(Q_BLOCKS) for hh in range(2)]

    def pv(tile, hh, p):
        pb = p.astype(BF16)
        out = None
        for b in range(TILE_BLOCKS):
            part = jnp.dot(vt_ref[0, tile * TILE_BLOCKS + b, hh], pb[b * bs:(b + 1) * bs],
                           preferred_element_type=F32)
            out = part if out is None else out + part
        return out

    qas = []
    for u, hh in items:
        i = g * Q_BLOCKS + u
        qs = q_ref[0, u * bs:(u + 1) * bs, hh * LANES:(hh + 1) * LANES]
        gate_t = _mm3(km_ref[0, hh, ATT_HEAD_DIM:, :], qs, NT_DIMS)
        bias = _topk_bias(gate_t, brow < i, browf, axis=0)
        bias = jnp.where(brow == i, 0.0, bias)
        bias_t = jnp.concatenate([jnp.full((ATT_HEAD_DIM, bs), NEG, F32), bias], axis=0)
        qas.append(jnp.where(low, qs, bias_t.T).astype(BF16))

    def scores(tile):
        off = pl.multiple_of(tile * tk, tk)
        return [lax.dot_general(k_ref[0, pl.ds(off, tk), hh * LANES:(hh + 1) * LANES], qas[it],
                                NT_DIMS, preferred_element_type=F32) for it, (u, hh) in enumerate(items)]

    for it, s in enumerate(scores(tl)):
        u, hh = items[it]
        causal = (krow + tl * tk) <= (qcol + (g * Q_BLOCKS + u) * bs)
        s = jnp.where(causal, s, NEG)
        m = jnp.max(s, axis=0, keepdims=True)
        m_ref[it] = m
        acc_ref[it] = pv(tl, hh, jnp.exp2(s - m))
    for it, s in enumerate(scores(jnp.maximum(tl - 1, 0))):
        s_ref[0, it] = s

    def body(t, carry):
        j = tl - 1 - t
        slot = t % 2
        cur = [s_ref[slot, it] for it in range(len(items))]
        nxt = scores(jnp.maximum(j - 1, 0))
        for it, (u, hh) in enumerate(items):
            m = m_ref[it]
            m_new = jnp.maximum(m, jnp.max(cur[it], axis=0, keepdims=True))
            acc_ref[it] = jnp.exp2(m - m_new) * acc_ref[it] + pv(j, hh, jnp.exp2(cur[it] - m_new))
            m_ref[it] = m_new
        for it in range(len(items)):
            s_ref[1 - slot, it] = nxt[it]
        return carry

    lax.fori_loop(0, tl, body, 0)
    d = ATT_HEAD_DIM
    for u in range(Q_BLOCKS):
        a0, a1 = acc_ref[2 * u], acc_ref[2 * u + 1]
        o = jnp.concatenate([a0[0:d] / a0[d:d + 1], a1[0:d] / a1[d:d + 1]], axis=0)
        o_ref[0, u * bs:(u + 1) * bs, :] = o.T


def _moba_prompt_call(q_aug, k_aug, vt, km_pad):
    n, t, _ = q_aug.shape
    nb = t // MOBA_BLOCK
    npair = ATT_WIDTH // LANES
    pair_w = 2 * LANES
    qrows = Q_BLOCKS * MOBA_BLOCK
    nitems = 2 * Q_BLOCKS
    return pl.pallas_call(
        _moba_prompt_kernel,
        grid=(n, npair, nb // Q_BLOCKS),
        in_specs=[
            pl.BlockSpec((1, qrows, pair_w), lambda n_, p_, i_: (n_, i_, p_)),
            pl.BlockSpec((1, t, pair_w), lambda n_, p_, i_: (n_, 0, p_)),
            pl.BlockSpec((1, nb, 2, VT_ROWS, MOBA_BLOCK), lambda n_, p_, i_: (n_, 0, p_, 0, 0)),
            pl.BlockSpec((1, 2, LANES, LANES), lambda n_, p_, i_: (n_, p_, 0, 0)),
        ],
        out_specs=pl.BlockSpec((1, qrows, LANES), lambda n_, p_, i_: (n_, i_, p_)),
        out_shape=jax.ShapeDtypeStruct((n, t, ATT_WIDTH), F32),
        scratch_shapes=[
            pltpu.VMEM((2, nitems, TILE_BLOCKS * MOBA_BLOCK, MOBA_BLOCK), F32),
            pltpu.VMEM((nitems, 1, MOBA_BLOCK), F32),
            pltpu.VMEM((nitems, VT_ROWS, MOBA_BLOCK), F32),
        ],
        compiler_params=_cparams(("arbitrary", "arbitrary", "arbitrary")),
        name="moba_prompt",
    )(q_aug, k_aug, vt, km_pad)


def _moba_sample_kernel(pt_ref, q_ref, kn_ref, vn_ref, *rest, npages, steps):
    pps = PAGES_PER_STEP
    kp = rest[:pps]
    vp = rest[pps:2 * pps]
    o_ref = rest[2 * pps]
    lg_ref, km_ref, acc_ref, lsum_ref = rest[2 * pps + 1:]
    s = pl.program_id(1)
    nsteps = npages // pps
    nblk = npages * PAGE_SIZE // MOBA_BLOCK
    ppb = MOBA_BLOCK // PAGE_SIZE
    hs = range(ATT_HEADS)
    half = TOK_ROWS // 2

    lane = lax.broadcasted_iota(jnp.int32, (1, LANES), 1)

    @pl.when(s == 0)
    def _():
        km_ref[...] = jnp.zeros(km_ref.shape, F32)

    @pl.when(s < nsteps)
    def _():
        qb = [q_ref[0, h].astype(BF16) for h in hs]
        for b in range(pps // ppb):
            page0 = s * pps + b * ppb
            for h in hs:
                kts = [kp[b * ppb + pg][h] for pg in range(ppb)]
                lg = jnp.dot(qb[h], jnp.concatenate(kts, axis=-1).astype(BF16),
                             preferred_element_type=F32)
                for pg in range(ppb):
                    lg_ref[page0 + pg, h] = lg[0:half, pg * PAGE_SIZE:(pg + 1) * PAGE_SIZE]
                ksum = kts[0]
                for pg in range(1, ppb):
                    ksum = ksum + kts[pg]
                col = jnp.sum(ksum, axis=-1, keepdims=True) * (1.0 / MOBA_BLOCK)
                km_ref[h] = jnp.where(lane == s * (pps // ppb) + b, col, km_ref[h])

    @pl.when(s == nsteps - 1)
    def _():
        rows = ATT_HEADS * half
        blk = lax.broadcasted_iota(jnp.int32, (rows, LANES), 1)
        trow = lax.broadcasted_iota(jnp.int32, (rows, TOK_ROWS), 0) % half
        tcol = lax.broadcasted_iota(jnp.int32, (rows, TOK_ROWS), 1)
        qf = [q_ref[0, h] for h in hs]
        gate = jnp.concatenate(
            [jnp.dot(qf[h][0:half], km_ref[h], precision=HIGHEST, preferred_element_type=F32)
             for h in hs], axis=0)
        bias = _topk_bias(gate, blk < nblk, blk.astype(F32))
        lo = jnp.concatenate(
            [lax.dot_general(qf[h].astype(BF16), kn_ref[0, h].astype(BF16), NT_DIMS,
                             preferred_element_type=F32)[0:half] for h in hs], axis=0)
        lo = jnp.where((tcol <= trow) & (tcol < steps), lo, NEG)
        bcols = [[bias[h * half:(h + 1) * half, b:b + 1] for b in range(nblk)] for h in hs]
        mts = []
        for h in hs:
            mt = jnp.full((half, PAGE_SIZE), NEG, F32)
            for b in range(nblk):
                for pg in range(ppb):
                    mt = jnp.maximum(mt, lg_ref[b * ppb + pg, h] + bcols[h][b])
            mts.append(mt)
        m = jnp.maximum(jnp.max(jnp.concatenate(mts, axis=0), axis=-1, keepdims=True),
                        jnp.max(lo, axis=-1, keepdims=True))
        po = jnp.exp2(lo - m)
        lts = []
        for h in hs:
            mh = m[h * half:(h + 1) * half]
            lt = jnp.zeros((half, PAGE_SIZE), F32)
            for b in range(nblk):
                for pg in range(ppb):
                    pb = jnp.exp2(lg_ref[b * ppb + pg, h] + bcols[h][b] - mh)
                    lg_ref[b * ppb + pg, h] = pb
                    lt = lt + pb
            lts.append(lt)
        lsum = (jnp.sum(jnp.concatenate(lts, axis=0), axis=-1, keepdims=True)
                + jnp.sum(po, axis=-1, keepdims=True))
        zeros = jnp.zeros((half, TOK_ROWS), F32)
        for h in hs:
            lsum_ref[h] = lsum[h * half:(h + 1) * half]
            po16 = jnp.concatenate([po[h * half:(h + 1) * half], zeros], axis=0).astype(BF16)
            acc_ref[h] = jnp.dot(po16, vn_ref[0, h].astype(BF16), preferred_element_type=F32)[0:half]

    @pl.when(s >= nsteps)
    def _():
        acc = [acc_ref[h] for h in hs]
        zeros = jnp.zeros((half, MOBA_BLOCK), F32)
        for b in range(pps // ppb):
            page0 = (s - nsteps) * pps + b * ppb
            for h in hs:
                pblk = jnp.concatenate([lg_ref[page0 + pg, h] for pg in range(ppb)], axis=-1)
                p16 = jnp.concatenate([pblk, zeros], axis=0).astype(BF16)
                vt = jnp.concatenate([vp[b * ppb + pg][h] for pg in range(ppb)], axis=-1).astype(BF16)
                acc[h] = acc[h] + lax.dot_general(p16, vt, NT_DIMS, preferred_element_type=F32)[0:half]
        for h in hs:
            acc_ref[h] = acc[h]

    @pl.when(s == 2 * nsteps - 1)
    def _():
        for h in hs:
            o_ref[0, h] = acc_ref[h] / lsum_ref[h]


def _moba_sample_call(page_flat, qh, knh, vnh, cache_k4, cache_v4, layer, npages, steps):
    batch = qh.shape[0]
    pps = PAGES_PER_STEP
    nsteps = npages // pps
    half = TOK_ROWS // 2
    page_block = (None, None, ATT_HEADS, ATT_HEAD_DIM, PAGE_SIZE)
    tok = pl.BlockSpec((1, ATT_HEADS, TOK_ROWS, ATT_HEAD_DIM), lambda n_, s_, pt: (n_, 0, 0, 0))

    def kspec(i):
        return pl.BlockSpec(
            page_block,
            lambda n_, s_, pt: (layer, pt[n_ * npages + jnp.minimum(s_, nsteps - 1) * pps + i], 0, 0, 0))

    def vspec(i):
        return pl.BlockSpec(
            page_block,
            lambda n_, s_, pt: (layer, pt[n_ * npages + jnp.maximum(s_ - nsteps, 0) * pps + i], 0, 0, 0))

    grid_spec = pltpu.PrefetchScalarGridSpec(
        num_scalar_prefetch=1,
        grid=(batch, 2 * nsteps),
        in_specs=[tok, tok, tok] + [kspec(i) for i in range(pps)] + [vspec(i) for i in range(pps)],
        out_specs=pl.BlockSpec((1, ATT_HEADS, half, ATT_HEAD_DIM), lambda n_, s_, pt: (n_, 0, 0, 0)),
        scratch_shapes=[
            pltpu.VMEM((npages, ATT_HEADS, half, PAGE_SIZE), F32),
            pltpu.VMEM((ATT_HEADS, ATT_HEAD_DIM, LANES), F32),
            pltpu.VMEM((ATT_HEADS, half, ATT_HEAD_DIM), F32),
            pltpu.VMEM((ATT_HEADS, half, 1), F32),
        ],
    )
    return pl.pallas_call(
        functools.partial(_moba_sample_kernel, npages=npages, steps=steps),
        grid_spec=grid_spec,
        out_shape=jax.ShapeDtypeStruct((batch, ATT_HEADS, half, ATT_HEAD_DIM), F32),
        compiler_params=_cparams(("arbitrary", "arbitrary")),
        name="moba_sample",
    )(page_flat, qh, knh, vnh, *([cache_k4] * pps), *([cache_v4] * pps))


def _gdn_kernel(q_ref, k_ref, v_ref, gb_ref, s0_ref, o_ref, sout_ref, s_scr, *, chunk, cps):
    c = pl.program_id(1)
    nc = pl.num_programs(1)

    @pl.when(c == 0)
    def _():
        s_scr[...] = s0_ref[0]

    ri = lax.broadcasted_iota(jnp.int32, (chunk, chunk), 0)
    ci = lax.broadcasted_iota(jnp.int32, (chunk, chunk), 1)
    tril = ri >= ci
    stril = ri > ci
    eye = (ri == ci).astype(F32)
    nsq = int(round(math.log2(chunk))) - 1

    def mm1(a, b, dims=NN_DIMS):
        return lax.dot_general(a.astype(BF16), b.astype(BF16), dims, preferred_element_type=F32)

    hs = range(DN_HEADS)
    cols = [slice(h * DN_HEAD_DIM, (h + 1) * DN_HEAD_DIM) for h in hs]
    items = [(cc, h) for cc in range(cps) for h in hs]
    rows = [slice(cc * chunk, (cc + 1) * chunk) for cc in range(cps)]
    gb = [gb_ref[0, rows[cc], :] for cc in range(cps)]
    gcum = [jnp.dot(tril.astype(F32), gb[cc], precision=HIGHEST, preferred_element_type=F32)
            for cc in range(cps)]
    gcum_rows = [lax.dot_general(gcum[cc], eye, TN_DIMS, precision=HIGHEST, preferred_element_type=F32)
                 for cc in range(cps)]
    q = {it: q_ref[0, rows[it[0]], cols[it[1]]] for it in items}
    k = {it: k_ref[0, rows[it[0]], cols[it[1]]] for it in items}
    v = {it: v_ref[0, rows[it[0]], cols[it[1]]] for it in items}
    beta = {(cc, h): gb[cc][:, h:h + 1] for cc, h in items}
    gc = {(cc, h): gcum[cc][:, DN_HEADS + h:DN_HEADS + h + 1] for cc, h in items}
    kbeta = {it: k[it] * beta[it] for it in items}
    kk = {it: _mm3(kbeta[it], k[it], NT_DIMS) for it in items}
    qk = {it: mm1(q[it], k[it], NT_DIMS) for it in items}
    decay = {}
    for cc, h in items:
        diff = gc[cc, h] - gcum_rows[cc][DN_HEADS + h:DN_HEADS + h + 1, :]
        decay[cc, h] = jnp.where(tril, jnp.exp(jnp.where(tril, diff, 0.0)), 0.0)
    eg = {it: jnp.exp(gc[it]) for it in items}
    pw = {it: -jnp.where(stril, kk[it] * decay[it], 0.0) for it in items}
    y = {it: jnp.concatenate([v[it] * beta[it], kbeta[it] * eg[it]], axis=-1) for it in items}
    y = {it: y[it] + _mm3(pw[it], y[it]) for it in items}
    for _ in range(nsq):
        pw = {it: _mm3(pw[it], pw[it]) for it in items}
        y = {it: y[it] + _mm3(pw[it], y[it]) for it in items}

    state = [s_scr[h] for h in hs]
    for cc in range(cps):
        its = [(cc, h) for h in hs]
        v_new = [y[it][:, 0:DN_HEAD_DIM] - _mm3(y[it][:, DN_HEAD_DIM:2 * DN_HEAD_DIM], state[it[1]])
                 for it in its]
        g_last = [gc[it][chunk - 1:chunk, :] for it in its]
        kv = [_mm3(k[it] * jnp.exp(g_last[h] - gc[it]), v_new[h], TN_DIMS) for h, it in enumerate(its)]
        for h, it in enumerate(its):
            attn = jnp.where(tril, qk[it] * decay[it], 0.0)
            o_ref[0, rows[cc], cols[h]] = mm1(q[it] * eg[it], state[h]) + mm1(attn, v_new[h])
        state = [state[h] * jnp.exp(g_last[h]) + kv[h] for h in hs]
    for h in hs:
        s_scr[h] = state[h]

    @pl.when(c == nc - 1)
    def _():
        sout_ref[0] = s_scr[...]


def _gdn_call(qd, kd, vd, gb, s0, s0_index, chunk, cps):
    n, t, _ = qd.shape
    step_rows = chunk * cps
    nc = t // step_rows
    row = lambda n_, c_: (n_, c_, 0)
    blk = pl.BlockSpec((1, step_rows, DN_WIDTH), row)
    sshape = (DN_HEADS, DN_HEAD_DIM, DN_HEAD_DIM)
    s0_block = (None,) * (s0.ndim - 4) + (1,) + sshape
    return pl.pallas_call(
        functools.partial(_gdn_kernel, chunk=chunk, cps=cps),
        grid=(n, nc),
        in_specs=[blk, blk, blk,
                  pl.BlockSpec((1, step_rows, LANES), row),
                  pl.BlockSpec(s0_block, lambda n_, c_: s0_index(n_))],
        out_specs=[blk, pl.BlockSpec((1,) + sshape, lambda n_, c_: (n_, 0, 0, 0))],
        out_shape=[jax.ShapeDtypeStruct((n, t, DN_WIDTH), F32),
                   jax.ShapeDtypeStruct((n,) + sshape, F32)],
        scratch_shapes=[pltpu.VMEM(sshape, F32)],
        compiler_params=_cparams(("arbitrary", "arbitrary")),
        name="gdn",
    )(qd, kd, vd, gb, s0)


def _out_kernel(oa_ref, sza_ref, od_ref, szd_ref, x_ref, gate_ref, wo_ref, nw_ref, lg_ref, lb_ref,
                o_ref, *, alpha):
    ya = (oa_ref[0] * sza_ref[0]).astype(BF16)
    od = od_ref[0]
    parts = []
    for h in range(DN_HEADS):
        oh = od[:, h * DN_HEAD_DIM:(h + 1) * DN_HEAD_DIM]
        ms = jnp.mean(oh * oh, axis=-1, keepdims=True)
        parts.append(oh * lax.rsqrt(ms + RMS_EPS) * nw_ref[...])
    yd = (jnp.concatenate(parts, axis=-1) * szd_ref[0]).astype(BF16)
    y = (jnp.dot(ya, wo_ref[0:ATT_WIDTH, :], preferred_element_type=F32)
         + jnp.dot(yd, wo_ref[ATT_WIDTH:ATT_WIDTH + DN_WIDTH, :], preferred_element_type=F32))
    r = alpha * x_ref[0] + (1.0 + gate_ref[0]) * y
    mu = jnp.mean(r, axis=-1, keepdims=True)
    rc = r - mu
    var = jnp.mean(rc * rc, axis=-1, keepdims=True)
    o_ref[0] = rc * lax.rsqrt(var + LN_EPS) * lg_ref[...] + lb_ref[...]


def _out_call(oa, sza, od, szd, x, gate, w_out_bf, nw, lg, lb, tm, alpha):
    n, t, d = x.shape
    nt = t // tm
    row = lambda n_, t_: (n_, t_, 0)
    full = lambda n_, t_: (0, 0)
    blk = pl.BlockSpec((1, tm, ATT_WIDTH), row)
    if gate.shape[1] == 1:
        gate_spec = pl.BlockSpec((1, 1, d), lambda n_, t_: (n_, 0, 0))
    else:
        gate_spec = pl.BlockSpec((1, tm, d), row)
    return pl.pallas_call(
        functools.partial(_out_kernel, alpha=alpha),
        grid=(n, nt),
        in_specs=[blk, blk, blk, blk,
                  pl.BlockSpec((1, tm, d), row),
                  gate_spec,
                  pl.BlockSpec((ATT_WIDTH + DN_WIDTH, d), full),
                  pl.BlockSpec((1, DN_HEAD_DIM), full),
                  pl.BlockSpec((1, d), full),
                  pl.BlockSpec((1, d), full)],
        out_specs=pl.BlockSpec((1, tm, d), row),
        out_shape=jax.ShapeDtypeStruct((n, t, d), F32),
        compiler_params=_cparams(("arbitrary", "arbitrary")),
        name="out_proj",
    )(oa, sza, od, szd, x, gate, w_out_bf, nw, lg, lb)


def _rope_tables(pos):
    half = ATT_HEAD_DIM // 2
    inv_freq = ROPE_THETA ** (-jnp.arange(half, dtype=F32) / half)
    ang = pos.astype(F32)[:, None] * inv_freq[None, :]
    cos = jnp.cos(ang)
    sin = jnp.sin(ang)
    reps = LANES // ATT_HEAD_DIM
    return jnp.tile(cos, (1, 2 * reps)), jnp.tile(jnp.concatenate([-sin, sin], axis=1), (1, reps))


def _lane_row(vec, offset):
    return jnp.zeros((1, LANES), F32).at[0, offset:offset + vec.shape[0]].set(vec.astype(F32))


def kernel(x_prompt, x_sample, c_prompt, c_sample, cache_k, cache_v, page_table, state_ssm, state_conv,
           w_ada, b_ada, w_in, conv_w, a_log, dt_bias, dn_norm_w, w_out, ln_g, ln_b):
    depth, d, _ = w_ada.shape
    bp, seq, _ = x_prompt.shape
    bs, dec, _ = x_sample.shape
    npages = page_table.shape[1]
    past = npages * PAGE_SIZE
    n_pool = cache_k.shape[1]
    alpha = float((2 * depth) ** 0.25)
    tm = MOBA_BLOCK
    nb = seq // tm
    assert seq % tm == 0 and past % MOBA_BLOCK == 0 and nb <= MAX_MOBA_BLOCKS and nb % TILE_BLOCKS == 0
    assert seq % (DN_CHUNK * GDN_CHUNKS_PER_STEP) == 0
    assert dec <= SUBLANES and npages % PAGES_PER_STEP == 0

    nc_rows = bp + bs
    c_rows = -(-nc_rows // SUBLANES) * SUBLANES
    c_all = jnp.zeros((c_rows, d), F32).at[:bp].set(c_prompt).at[bp:nc_rows].set(c_sample)
    mod_all = _mod_call(c_all, w_ada, b_ada)

    cos_p, sin_p = _rope_tables(jnp.arange(seq, dtype=jnp.int32))
    pos_s = past + jnp.repeat(jnp.arange(dec, dtype=jnp.int32), bs)
    cos_s, sin_s = _rope_tables(pos_s)

    cache_kt = cache_k.transpose(0, 1, 3, 4, 2)
    cache_vt = cache_v.transpose(0, 1, 3, 4, 2)
    page_flat = page_table.reshape(-1).astype(jnp.int32)
    zero_state = jnp.zeros((bp, DN_HEADS, DN_HEAD_DIM, DN_HEAD_DIM), F32)

    def to_bm8(a):
        w = a.shape[-1]
        a = a.reshape(dec, bs, w).transpose(1, 0, 2)
        return jnp.pad(a, ((0, 0), (0, SUBLANES - dec), (0, 0)))

    def to_heads(a):
        a = a.reshape(dec, bs, ATT_HEADS, ATT_HEAD_DIM).transpose(1, 2, 0, 3)
        return jnp.pad(a, ((0, 0), (0, 0), (0, TOK_ROWS - dec), (0, 0)))

    xp = x_prompt
    xs_tm = x_sample.transpose(1, 0, 2).reshape(dec * bs, d)
    outs = [[] for _ in range(8)]
    for l in range(depth):
        w_main = w_in[l, :, :MAIN_COLS].astype(BF16)
        w_small = jnp.zeros((d, LANES), F32).at[:, :2 * DN_HEADS].set(w_in[l, :, MAIN_COLS:])
        alog = _lane_row(a_log[l], DN_HEADS)
        dtb = _lane_row(dt_bias[l], DN_HEADS)
        w_out_bf = w_out[l].astype(BF16)
        nw = dn_norm_w[l].reshape(1, DN_HEAD_DIM)
        lg = ln_g[l].reshape(1, d)
        lb = ln_b[l].reshape(1, d)
        mod_p = mod_all[l, :bp].reshape(bp, 1, 3 * d)
        mod_s = mod_all[l, bp:nc_rows]

        (q_aug, k, v, k_aug, vt, kmean, sza, qd, kd, vd, szd, gb, nconv) = _in_prompt_call(
            xp, mod_p, w_main, w_small, conv_w[l], alog, dtb, cos_p, sin_p, tm)
        km_heads = kmean.reshape(bp, nb, ATT_HEADS, ATT_HEAD_DIM).transpose(0, 2, 1, 3)
        km_pad = jnp.pad(km_heads, ((0, 0), (0, 0), (ATT_HEAD_DIM, LANES - ATT_HEAD_DIM - nb),
                                    (0, LANES - ATT_HEAD_DIM)))
        oa = _moba_prompt_call(q_aug, k_aug, vt, km_pad)
        od, s_new = _gdn_call(qd, kd, vd, gb, zero_state, lambda n_: (n_, 0, 0, 0), DN_CHUNK,
                              GDN_CHUNKS_PER_STEP)
        xp = _out_call(oa, sza, od, szd, xp, mod_p[:, :, 2 * d:], w_out_bf, nw, lg, lb, tm, alpha)
        outs[0].append(k.reshape(bp, seq, ATT_HEADS, ATT_HEAD_DIM))
        outs[1].append(v.reshape(bp, seq, ATT_HEADS, ATT_HEAD_DIM))
        outs[2].append(s_new)
        outs[3].append(nconv[:, SUBLANES - (CONV_WIDTH - 1):, :])

        mod_rows = jnp.tile(mod_s, (dec, 1))
        hist_tm = state_conv[l].transpose(1, 0, 2).reshape((CONV_WIDTH - 1) * bs, DN_QKV)
        (q, k, v, sza, qd, kd, vd, szd, gb, nconv) = _in_sample_call(
            xs_tm, mod_rows[:, :d], mod_rows[:, d:2 * d], w_main, w_small, conv_w[l], alog, dtb,
            cos_s, sin_s, hist_tm, bs, dec)
        k8, v8 = to_bm8(k), to_bm8(v)
        oh = _moba_sample_call(page_flat, to_heads(q), to_heads(k), to_heads(v), cache_kt, cache_vt,
                               l, npages, dec)
        oa8 = oh.transpose(0, 2, 1, 3).reshape(bs, SUBLANES, ATT_WIDTH)
        od8, s_new = _gdn_call(to_bm8(qd), to_bm8(kd), to_bm8(vd), to_bm8(gb), state_ssm,
                               lambda n_, l=l: (l, n_, 0, 0, 0), SUBLANES, 1)
        rows8 = bs * SUBLANES
        flat = lambda a: a.reshape(1, rows8, a.shape[-1])
        gate_rows = jnp.repeat(mod_s[:, 2 * d:], SUBLANES, axis=0).reshape(1, rows8, d)
        xs8 = _out_call(flat(oa8), flat(to_bm8(sza)), flat(od8), flat(to_bm8(szd)), flat(to_bm8(xs_tm)),
                        gate_rows, w_out_bf, nw, lg, lb, rows8, alpha)
        xs_bm = xs8.reshape(bs, SUBLANES, d)[:, :dec]
        xs_tm = xs_bm.transpose(1, 0, 2).reshape(dec * bs, d)
        outs[4].append(k8[:, :dec].reshape(bs, dec, ATT_HEADS, ATT_HEAD_DIM))
        outs[5].append(v8[:, :dec].reshape(bs, dec, ATT_HEADS, ATT_HEAD_DIM))
        outs[6].append(s_new)
        outs[7].append(nconv.reshape(CONV_WIDTH - 1, bs, DN_QKV).transpose(1, 0, 2))

    stacked = [jnp.stack(o) for o in outs]
    return (xp, xs_bm, stacked[0], stacked[1], stacked[2], stacked[3],
            stacked[4], stacked[5], stacked[6], stacked[7])
```

```python
import functools
import math

import jax
import jax.numpy as jnp
from jax import lax
from jax.experimental import pallas as pl
from jax.experimental.pallas import tpu as pltpu

F32 = jnp.float32
BF16 = jnp.bfloat16
HIGHEST = lax.Precision.HIGHEST

ATT_HEADS = 8
ATT_HEAD_DIM = 64
ATT_WIDTH = ATT_HEADS * ATT_HEAD_DIM
DN_HEADS = 4
DN_HEAD_DIM = 128
DN_WIDTH = DN_HEADS * DN_HEAD_DIM
DN_QKV = 3 * DN_WIDTH
MAIN_COLS = 4 * ATT_WIDTH + DN_QKV + DN_WIDTH
MOBA_BLOCK = 256
MOBA_TOPK = 3
DN_CHUNK = 64
GDN_CHUNKS_PER_STEP = 4
CONV_WIDTH = 4
PAGE_SIZE = 128
ROPE_THETA = 10000.0
LN_EPS = 1e-5
RMS_EPS = 1e-6
L2_EPS = 1e-6
ATT_SCALE = ATT_HEAD_DIM ** -0.5
QK_SCALE = ATT_SCALE * math.log2(math.e)
NEG = -1e30
LANES = 128
SUBLANES = 8
TOK_ROWS = 2 * SUBLANES
VT_ROWS = ATT_HEAD_DIM + 16
MAX_MOBA_BLOCKS = LANES - ATT_HEAD_DIM
TILE_BLOCKS = 4
Q_BLOCKS = 1
assert TILE_BLOCKS % Q_BLOCKS == 0
PAGES_PER_STEP = 32
VMEM_LIMIT = 56 * 1024 * 1024

NN_DIMS = (((1,), (0,)), ((), ()))
NT_DIMS = (((1,), (1,)), ((), ()))
TN_DIMS = (((0,), (0,)), ((), ()))


def _sigmoid(x):
    return 1.0 / (1.0 + jnp.exp(-x))


def _silu(x):
    return x * _sigmoid(x)


def _softplus(x):
    return jnp.maximum(x, 0.0) + jnp.log1p(jnp.exp(-jnp.abs(x)))


def _split_bf16(a):
    hi = a.astype(BF16)
    return hi, (a - hi.astype(F32)).astype(BF16)


def _mm3(a, b, dims=NN_DIMS):
    ah, al = _split_bf16(a)
    bh, bl = _split_bf16(b)

    def dot(x, y):
        return lax.dot_general(x, y, dims, preferred_element_type=F32)

    return (dot(ah, bl) + dot(al, bh)) + dot(ah, bh)


def _cparams(sem):
    return pltpu.CompilerParams(dimension_semantics=sem, vmem_limit_bytes=VMEM_LIMIT)


def _mod_kernel(c_ref, w_ref, b_ref, o_ref):
    s = _silu(c_ref[...])
    o_ref[0] = jnp.dot(s.astype(BF16), w_ref[0].astype(BF16), preferred_element_type=F32) + b_ref[0]


def _mod_call(c_all, w_ada, b_ada):
    depth, d, d3 = w_ada.shape
    rows = c_all.shape[0]
    nj = d3 // d
    return pl.pallas_call(
        _mod_kernel,
        grid=(depth, nj),
        in_specs=[
            pl.BlockSpec((rows, d), lambda l, j: (0, 0)),
            pl.BlockSpec((1, d, d), lambda l, j: (l, 0, j)),
            pl.BlockSpec((1, 1, d), lambda l, j: (l, 0, j)),
        ],
        out_specs=pl.BlockSpec((1, rows, d), lambda l, j: (l, 0, j)),
        out_shape=jax.ShapeDtypeStruct((depth, rows, d3), F32),
        compiler_params=_cparams(("arbitrary", "arbitrary")),
        name="mod",
    )(c_all, w_ada, b_ada.reshape(depth, 1, d3))


def _rope(p, cos, sin_signed):
    lane = lax.broadcasted_iota(jnp.int32, (1, LANES), 1)
    first_half = (lane % ATT_HEAD_DIM) < (ATT_HEAD_DIM // 2)
    outs = []
    for j in range(ATT_WIDTH // LANES):
        xj = p[:, j * LANES:(j + 1) * LANES]
        partner = jnp.where(first_half, pltpu.roll(xj, LANES - 32, 1), pltpu.roll(xj, 32, 1))
        outs.append(xj * cos + partner * sin_signed)
    return jnp.concatenate(outs, axis=-1)


def _l2norm_heads(c, scale):
    outs = []
    for h in range(DN_HEADS):
        ch = c[:, h * DN_HEAD_DIM:(h + 1) * DN_HEAD_DIM]
        ss = jnp.sum(ch * ch, axis=-1, keepdims=True)
        outs.append(ch * (lax.rsqrt(ss + L2_EPS) * scale))
    return jnp.concatenate(outs, axis=-1)


def _beta_decay(h, ws_ref, alog_ref, dtb_ref):
    raw = _mm3(h, ws_ref[...])
    lane = lax.broadcasted_iota(jnp.int32, (1, LANES), 1)
    beta = _sigmoid(raw)
    g = -jnp.exp(alog_ref[...]) * _softplus(raw + dtb_ref[...])
    return jnp.where(lane < DN_HEADS, beta, jnp.where(lane < 2 * DN_HEADS, g, 0.0))


def _proj(hb, w_ref, g):
    return jnp.dot(hb, w_ref[:, g * ATT_WIDTH:(g + 1) * ATT_WIDTH], preferred_element_type=F32)


def _head_slabs(p, fill):
    lane = lax.broadcasted_iota(jnp.int32, (1, LANES), 1)
    low = lane < ATT_HEAD_DIM
    outs = []
    for j in range(ATT_WIDTH // LANES):
        xj = p[:, j * LANES:(j + 1) * LANES]
        outs.append(jnp.where(low, xj, fill))
        outs.append(jnp.where(low, pltpu.roll(xj, ATT_HEAD_DIM, 1), fill))
    return jnp.concatenate(outs, axis=-1)


def _in_prompt_kernel(x_ref, mod_ref, w_ref, ws_ref, cw_ref, alog_ref, dtb_ref, cos_ref, sin_ref,
                      qa_ref, k_ref, v_ref, ka_ref, vt_ref, km_ref, sza_ref,
                      qd_ref, kd_ref, vd_ref, szd_ref, gb_ref, nc_ref, xbuf, *, tm):
    t = pl.program_id(1)
    d = x_ref.shape[-1]
    hist = SUBLANES

    @pl.when(t == 0)
    def _():
        xbuf[0:hist, :] = jnp.zeros((hist, DN_QKV), F32)

    x = x_ref[0]
    shift = mod_ref[0][:, 0:d]
    scale = mod_ref[0][:, d:2 * d]
    h = x * (1.0 + scale) + shift
    hb = h.astype(BF16)
    cos = cos_ref[...]
    sin = sin_ref[...]

    lane = lax.broadcasted_iota(jnp.int32, (1, LANES), 1)
    qa_ref[0] = _head_slabs(_rope(_proj(hb, w_ref, 0), cos, sin) * QK_SCALE, 0.0)
    kr = _rope(_proj(hb, w_ref, 1), cos, sin)
    k_ref[0] = kr
    block_onehot = (lane == ATT_HEAD_DIM + t).astype(F32)
    ka_ref[0] = _head_slabs(kr, block_onehot).astype(BF16)
    km_ref[0, 0] = jnp.sum(kr, axis=0, keepdims=True) * (1.0 / tm)
    v = _proj(hb, w_ref, 2)
    v_ref[0] = v
    vt = v.T
    for hd in range(ATT_HEADS):
        vt_ref[0, 0, hd, 0:ATT_HEAD_DIM, :] = vt[hd * ATT_HEAD_DIM:(hd + 1) * ATT_HEAD_DIM, :].astype(BF16)
        vt_ref[0, 0, hd, ATT_HEAD_DIM:VT_ROWS, :] = jnp.ones((VT_ROWS - ATT_HEAD_DIM, tm), BF16)
    sza_ref[0] = _silu(_proj(hb, w_ref, 3))
    szd_ref[0] = _silu(_proj(hb, w_ref, 7))
    gb_ref[0] = _beta_decay(h, ws_ref, alog_ref, dtb_ref)

    outs = (qd_ref, kd_ref, vd_ref)
    for gi in range(3):
        cols = slice(gi * DN_WIDTH, (gi + 1) * DN_WIDTH)
        xbuf[hist:hist + tm, cols] = _proj(hb, w_ref, 4 + gi)
        conv = jnp.zeros((tm, DN_WIDTH), F32)
        for i in range(CONV_WIDTH):
            off = hist - (CONV_WIDTH - 1) + i
            conv = conv + xbuf[off:off + tm, cols] * cw_ref[i:i + 1, cols]
        c = _silu(conv)
        if gi == 0:
            c = _l2norm_heads(c, DN_HEAD_DIM ** -0.5)
        elif gi == 1:
            c = _l2norm_heads(c, 1.0)
        outs[gi][0] = c

    tail = xbuf[tm:tm + hist, :]
    nc_ref[0] = tail
    xbuf[0:hist, :] = tail


def _in_prompt_call(x, mod, w_main, w_small, conv_w, alog, dtb, cos, sin, tm):
    n, t, d = x.shape
    nt = t // tm
    row = lambda n_, t_: (n_, t_, 0)
    full = lambda n_, t_: (0, 0)
    wide = jax.ShapeDtypeStruct((n, t, ATT_WIDTH), F32)
    blk = pl.BlockSpec((1, tm, ATT_WIDTH), row)
    slab_w = ATT_HEADS * LANES
    slab_blk = pl.BlockSpec((1, tm, slab_w), row)
    return pl.pallas_call(
        functools.partial(_in_prompt_kernel, tm=tm),
        grid=(n, nt),
        in_specs=[
            pl.BlockSpec((1, tm, d), row),
            pl.BlockSpec((1, 1, 3 * d), lambda n_, t_: (n_, 0, 0)),
            pl.BlockSpec((d, MAIN_COLS), full),
            pl.BlockSpec((d, LANES), full),
            pl.BlockSpec((CONV_WIDTH, DN_QKV), full),
            pl.BlockSpec((1, LANES), full),
            pl.BlockSpec((1, LANES), full),
            pl.BlockSpec((tm, LANES), lambda n_, t_: (t_, 0)),
            pl.BlockSpec((tm, LANES), lambda n_, t_: (t_, 0)),
        ],
        out_specs=[
            slab_blk, blk, blk, slab_blk,
            pl.BlockSpec((1, 1, ATT_HEADS, VT_ROWS, tm), lambda n_, t_: (n_, t_, 0, 0, 0)),
            pl.BlockSpec((1, 1, 1, ATT_WIDTH), lambda n_, t_: (n_, t_, 0, 0)),
            blk, blk, blk, blk, blk,
            pl.BlockSpec((1, tm, LANES), row),
            pl.BlockSpec((1, SUBLANES, DN_QKV), lambda n_, t_: (n_, 0, 0)),
        ],
        out_shape=[
            jax.ShapeDtypeStruct((n, t, slab_w), F32), wide, wide,
            jax.ShapeDtypeStruct((n, t, slab_w), BF16),
            jax.ShapeDtypeStruct((n, nt, ATT_HEADS, VT_ROWS, tm), BF16),
            jax.ShapeDtypeStruct((n, nt, 1, ATT_WIDTH), F32),
            wide, wide, wide, wide, wide,
            jax.ShapeDtypeStruct((n, t, LANES), F32),
            jax.ShapeDtypeStruct((n, SUBLANES, DN_QKV), F32),
        ],
        scratch_shapes=[pltpu.VMEM((tm + SUBLANES, DN_QKV), F32)],
        compiler_params=_cparams(("arbitrary", "arbitrary")),
        name="in_prompt",
    )(x, mod, w_main, w_small, conv_w, alog, dtb, cos, sin)


def _in_sample_kernel(x_ref, shift_ref, scale_ref, w_ref, ws_ref, cw_ref, alog_ref, dtb_ref,
                      cos_ref, sin_ref, hist_ref,
                      q_ref, k_ref, v_ref, sza_ref, qd_ref, kd_ref, vd_ref, szd_ref, gb_ref, nc_ref,
                      xall, *, batch, steps):
    rows = batch * steps
    nh = (CONV_WIDTH - 1) * batch
    h = x_ref[...] * (1.0 + scale_ref[...]) + shift_ref[...]
    hb = h.astype(BF16)
    cos = cos_ref[...]
    sin = sin_ref[...]
    q_ref[...] = _rope(_proj(hb, w_ref, 0), cos, sin) * QK_SCALE
    k_ref[...] = _rope(_proj(hb, w_ref, 1), cos, sin)
    v_ref[...] = _proj(hb, w_ref, 2)
    sza_ref[...] = _silu(_proj(hb, w_ref, 3))
    szd_ref[...] = _silu(_proj(hb, w_ref, 7))
    gb_ref[...] = _beta_decay(h, ws_ref, alog_ref, dtb_ref)

    xall[0:nh, :] = hist_ref[...]
    outs = (qd_ref, kd_ref, vd_ref)
    for gi in range(3):
        cols = slice(gi * DN_WIDTH, (gi + 1) * DN_WIDTH)
        xall[nh:nh + rows, cols] = _proj(hb, w_ref, 4 + gi)
        conv = jnp.zeros((rows, DN_WIDTH), F32)
        for i in range(CONV_WIDTH):
            conv = conv + xall[i * batch:i * batch + rows, cols] * cw_ref[i:i + 1, cols]
        c = _silu(conv)
        if gi == 0:
            c = _l2norm_heads(c, DN_HEAD_DIM ** -0.5)
        elif gi == 1:
            c = _l2norm_heads(c, 1.0)
        outs[gi][...] = c
    nc_ref[...] = xall[rows:rows + nh, :]


def _in_sample_call(x_tm, shift, scale, w_main, w_small, conv_w, alog, dtb, cos, sin, hist_tm, batch, steps):
    rows = batch * steps
    nh = (CONV_WIDTH - 1) * batch
    wide = jax.ShapeDtypeStruct((rows, ATT_WIDTH), F32)
    return pl.pallas_call(
        functools.partial(_in_sample_kernel, batch=batch, steps=steps),
        out_shape=[wide] * 8 + [jax.ShapeDtypeStruct((rows, LANES), F32),
                                jax.ShapeDtypeStruct((nh, DN_QKV), F32)],
        scratch_shapes=[pltpu.VMEM((rows + nh, DN_QKV), F32)],
        compiler_params=pltpu.CompilerParams(vmem_limit_bytes=VMEM_LIMIT),
        name="in_sample",
    )(x_tm, shift, scale, w_main, w_small, conv_w, alog, dtb, cos, sin, hist_tm)


def _topk_bias(gate, valid, blkf, axis=-1):
    gate = jnp.where(valid, gate, -jnp.inf)
    bias = jnp.full(gate.shape, NEG, F32)
    for _ in range(MOBA_TOPK):
        mx = jnp.max(gate, axis=axis, keepdims=True)
        cand = (gate == mx) & (mx > -jnp.inf)
        idx = jnp.min(jnp.where(cand, blkf, 1e9), axis=axis, keepdims=True)
        pick = blkf == idx
        bias = jnp.where(pick, 0.0, bias)
        gate = jnp.where(pick, -jnp.inf, gate)
    return bias


def _moba_prompt_kernel(q_ref, k_ref, vt_ref, km_ref, o_ref, s_ref, m_ref, acc_ref):
    g = pl.program_id(2)
    bs = MOBA_BLOCK
    lane = lax.broadcasted_iota(jnp.int32, (1, LANES), 1)
    low = lane < ATT_HEAD_DIM
    brow = lax.broadcasted_iota(jnp.int32, (MAX_MOBA_BLOCKS, bs), 0)
    browf = brow.astype(F32)
    tk = TILE_BLOCKS * bs
    tl = (g * Q_BLOCKS) // TILE_BLOCKS
    krow = lax.broadcasted_iota(jnp.int32, (tk, bs), 0)
    qcol = lax.broadcasted_iota(jnp.int32, (tk, bs), 1)
    items = [(u, hh) for u in range(Q_BLOCKS) for hh in range(2)]

    def pv(tile, hh, p):
        pb = p.astype(BF16)
        out = None
        for b in range(TILE_BLOCKS):
            part = jnp.dot(vt_ref[0, tile * TILE_BLOCKS + b, hh], pb[b * bs:(b + 1) * bs],
                           preferred_element_type=F32)
            out = part if out is None else out + part
        return out

    qas = []
    for u, hh in items:
        i = g * Q_BLOCKS + u
        qs = q_ref[0, u * bs:(u + 1) * bs, hh * LANES:(hh + 1) * LANES]
        gate_t = _mm3(km_ref[0, hh, ATT_HEAD_DIM:, :], qs, NT_DIMS)
        bias = _topk_bias(gate_t, brow < i, browf, axis=0)
        bias = jnp.where(brow == i, 0.0, bias)
        bias_t = jnp.concatenate([jnp.full((ATT_HEAD_DIM, bs), NEG, F32), bias], axis=0)
        qas.append(jnp.where(low, qs, bias_t.T).astype(BF16))

    def scores(tile):
        off = pl.multiple_of(tile * tk, tk)
        return [lax.dot_general(k_ref[0, pl.ds(off, tk), hh * LANES:(hh + 1) * LANES], qas[it],
                                NT_DIMS, preferred_element_type=F32) for it, (u, hh) in enumerate(items)]

    for it, s in enumerate(scores(tl)):
        u, hh = items[it]
        causal = (krow + tl * tk) <= (qcol + (g * Q_BLOCKS + u) * bs)
        s = jnp.where(causal, s, NEG)
        m = jnp.max(s, axis=0, keepdims=True)
        m_ref[it] = m
        acc_ref[it] = pv(tl, hh, jnp.exp2(s - m))
    for it, s in enumerate(scores(jnp.maximum(tl - 1, 0))):
        s_ref[0, it] = s

    def body(t, carry):
        j = tl - 1 - t
        slot = t % 2
        cur = [s_ref[slot, it] for it in range(len(items))]
        nxt = scores(jnp.maximum(j - 1, 0))
        for it, (u, hh) in enumerate(items):
            m = m_ref[it]
            m_new = jnp.maximum(m, jnp.max(cur[it], axis=0, keepdims=True))
            acc_ref[it] = jnp.exp2(m - m_new) * acc_ref[it] + pv(j, hh, jnp.exp2(cur[it] - m_new))
            m_ref[it] = m_new
        for it in range(len(items)):
            s_ref[1 - slot, it] = nxt[it]
        return carry

    lax.fori_loop(0, tl, body, 0)
    d = ATT_HEAD_DIM
    for u in range(Q_BLOCKS):
        a0, a1 = acc_ref[2 * u], acc_ref[2 * u + 1]
        o = jnp.concatenate([a0[0:d] / a0[d:d + 1], a1[0:d] / a1[d:d + 1]], axis=0)
        o_ref[0, u * bs:(u + 1) * bs, :] = o.T


def _moba_prompt_call(q_aug, k_aug, vt, km_pad):
    n, t, _ = q_aug.shape
    nb = t // MOBA_BLOCK
    npair = ATT_WIDTH // LANES
    pair_w = 2 * LANES
    qrows = Q_BLOCKS * MOBA_BLOCK
    nitems = 2 * Q_BLOCKS
    return pl.pallas_call(
        _moba_prompt_kernel,
        grid=(n, npair, nb // Q_BLOCKS),
        in_specs=[
            pl.BlockSpec((1, qrows, pair_w), lambda n_, p_, i_: (n_, i_, p_)),
            pl.BlockSpec((1, t, pair_w), lambda n_, p_, i_: (n_, 0, p_)),
            pl.BlockSpec((1, nb, 2, VT_ROWS, MOBA_BLOCK), lambda n_, p_, i_: (n_, 0, p_, 0, 0)),
            pl.BlockSpec((1, 2, LANES, LANES), lambda n_, p_, i_: (n_, p_, 0, 0)),
        ],
        out_specs=pl.BlockSpec((1, qrows, LANES), lambda n_, p_, i_: (n_, i_, p_)),
        out_shape=jax.ShapeDtypeStruct((n, t, ATT_WIDTH), F32),
        scratch_shapes=[
            pltpu.VMEM((2, nitems, TILE_BLOCKS * MOBA_BLOCK, MOBA_BLOCK), F32),
            pltpu.VMEM((nitems, 1, MOBA_BLOCK), F32),
            pltpu.VMEM((nitems, VT_ROWS, MOBA_BLOCK), F32),
        ],
        compiler_params=_cparams(("arbitrary", "arbitrary", "arbitrary")),
        name="moba_prompt",
    )(q_aug, k_aug, vt, km_pad)


def _moba_sample_kernel(pt_ref, q_ref, kn_ref, vn_ref, *rest, npages, steps):
    pps = PAGES_PER_STEP
    kp = rest[:pps]
    vp = rest[pps:2 * pps]
    o_ref = rest[2 * pps]
    lg_ref, km_ref, acc_ref, lsum_ref = rest[2 * pps + 1:]
    s = pl.program_id(1)
    nsteps = npages // pps
    nblk = npages * PAGE_SIZE // MOBA_BLOCK
    ppb = MOBA_BLOCK // PAGE_SIZE
    hs = range(ATT_HEADS)
    half = TOK_ROWS // 2

    lane = lax.broadcasted_iota(jnp.int32, (1, LANES), 1)

    @pl.when(s == 0)
    def _():
        km_ref[...] = jnp.zeros(km_ref.shape, F32)

    @pl.when(s < nsteps)
    def _():
        qb = [q_ref[0, h].astype(BF16) for h in hs]
        for b in range(pps // ppb):
            page0 = s * pps + b * ppb
            for h in hs:
                kts = [kp[b * ppb + pg][h] for pg in range(ppb)]
                lg = jnp.dot(qb[h], jnp.concatenate(kts, axis=-1).astype(BF16),
                             preferred_element_type=F32)
                for pg in range(ppb):
                    lg_ref[page0 + pg, h] = lg[0:half, pg * PAGE_SIZE:(pg + 1) * PAGE_SIZE]
                ksum = kts[0]
                for pg in range(1, ppb):
                    ksum = ksum + kts[pg]
                col = jnp.sum(ksum, axis=-1, keepdims=True) * (1.0 / MOBA_BLOCK)
                km_ref[h] = jnp.where(lane == s * (pps // ppb) + b, col, km_ref[h])

    @pl.when(s == nsteps - 1)
    def _():
        rows = ATT_HEADS * half
        blk = lax.broadcasted_iota(jnp.int32, (rows, LANES), 1)
        trow = lax.broadcasted_iota(jnp.int32, (rows, TOK_ROWS), 0) % half
        tcol = lax.broadcasted_iota(jnp.int32, (rows, TOK_ROWS), 1)
        qf = [q_ref[0, h] for h in hs]
        gate = jnp.concatenate(
            [jnp.dot(qf[h][0:half], km_ref[h], precision=HIGHEST, preferred_element_type=F32)
             for h in hs], axis=0)
        bias = _topk_bias(gate, blk < nblk, blk.astype(F32))
        lo = jnp.concatenate(
            [lax.dot_general(qf[h].astype(BF16), kn_ref[0, h].astype(BF16), NT_DIMS,
                             preferred_element_type=F32)[0:half] for h in hs], axis=0)
        lo = jnp.where((tcol <= trow) & (tcol < steps), lo, NEG)
        bcols = [[bias[h * half:(h + 1) * half, b:b + 1] for b in range(nblk)] for h in hs]
        mts = []
        for h in hs:
            mt = jnp.full((half, PAGE_SIZE), NEG, F32)
            for b in range(nblk):
                for pg in range(ppb):
                    mt = jnp.maximum(mt, lg_ref[b * ppb + pg, h] + bcols[h][b])
            mts.append(mt)
        m = jnp.maximum(jnp.max(jnp.concatenate(mts, axis=0), axis=-1, keepdims=True),
                        jnp.max(lo, axis=-1, keepdims=True))
        po = jnp.exp2(lo - m)
        lts = []
        for h in hs:
            mh = m[h * half:(h + 1) * half]
            lt = jnp.zeros((half, PAGE_SIZE), F32)
            for b in range(nblk):
                for pg in range(ppb):
                    pb = jnp.exp2(lg_ref[b * ppb + pg, h] + bcols[h][b] - mh)
                    lg_ref[b * ppb + pg, h] = pb
                    lt = lt + pb
            lts.append(lt)
        lsum = (jnp.sum(jnp.concatenate(lts, axis=0), axis=-1, keepdims=True)
                + jnp.sum(po, axis=-1, keepdims=True))
        zeros = jnp.zeros((half, TOK_ROWS), F32)
        for h in hs:
            lsum_ref[h] = lsum[h * half:(h + 1) * half]
            po16 = jnp.concatenate([po[h * half:(h + 1) * half], zeros], axis=0).astype(BF16)
            acc_ref[h] = jnp.dot(po16, vn_ref[0, h].astype(BF16), preferred_element_type=F32)[0:half]

    @pl.when(s >= nsteps)
    def _():
        acc = [acc_ref[h] for h in hs]
        zeros = jnp.zeros((half, MOBA_BLOCK), F32)
        for b in range(pps // ppb):
            page0 = (s - nsteps) * pps + b * ppb
            for h in hs:
                pblk = jnp.concatenate([lg_ref[page0 + pg, h] for pg in range(ppb)], axis=-1)
                p16 = jnp.concatenate([pblk, zeros], axis=0).astype(BF16)
                vt = jnp.concatenate([vp[b * ppb + pg][h] for pg in range(ppb)], axis=-1).astype(BF16)
                acc[h] = acc[h] + lax.dot_general(p16, vt, NT_DIMS, preferred_element_type=F32)[0:half]
        for h in hs:
            acc_ref[h] = acc[h]

    @pl.when(s == 2 * nsteps - 1)
    def _():
        for h in hs:
            o_ref[0, h] = acc_ref[h] / lsum_ref[h]


def _moba_sample_call(page_flat, qh, knh, vnh, cache_k4, cache_v4, layer, npages, steps):
    batch = qh.shape[0]
    pps = PAGES_PER_STEP
    nsteps = npages // pps
    half = TOK_ROWS // 2
    page_block = (None, None, ATT_HEADS, ATT_HEAD_DIM, PAGE_SIZE)
    tok = pl.BlockSpec((1, ATT_HEADS, TOK_ROWS, ATT_HEAD_DIM), lambda n_, s_, pt: (n_, 0, 0, 0))

    def kspec(i):
        return pl.BlockSpec(
            page_block,
            lambda n_, s_, pt: (layer, pt[n_ * npages + jnp.minimum(s_, nsteps - 1) * pps + i], 0, 0, 0))

    def vspec(i):
        return pl.BlockSpec(
            page_block,
            lambda n_, s_, pt: (layer, pt[n_ * npages + jnp.maximum(s_ - nsteps, 0) * pps + i], 0, 0, 0))

    grid_spec = pltpu.PrefetchScalarGridSpec(
        num_scalar_prefetch=1,
        grid=(batch, 2 * nsteps),
        in_specs=[tok, tok, tok] + [kspec(i) for i in range(pps)] + [vspec(i) for i in range(pps)],
        out_specs=pl.BlockSpec((1, ATT_HEADS, half, ATT_HEAD_DIM), lambda n_, s_, pt: (n_, 0, 0, 0)),
        scratch_shapes=[
            pltpu.VMEM((npages, ATT_HEADS, half, PAGE_SIZE), F32),
            pltpu.VMEM((ATT_HEADS, ATT_HEAD_DIM, LANES), F32),
            pltpu.VMEM((ATT_HEADS, half, ATT_HEAD_DIM), F32),
            pltpu.VMEM((ATT_HEADS, half, 1), F32),
        ],
    )
    return pl.pallas_call(
        functools.partial(_moba_sample_kernel, npages=npages, steps=steps),
        grid_spec=grid_spec,
        out_shape=jax.ShapeDtypeStruct((batch, ATT_HEADS, half, ATT_HEAD_DIM), F32),
        compiler_params=_cparams(("arbitrary", "arbitrary")),
        name="moba_sample",
    )(page_flat, qh, knh, vnh, *([cache_k4] * pps), *([cache_v4] * pps))


def _gdn_kernel(q_ref, k_ref, v_ref, gb_ref, s0_ref, o_ref, sout_ref, s_scr, *, chunk, cps):
    c = pl.program_id(1)
    nc = pl.num_programs(1)

    @pl.when(c == 0)
    def _():
        s_scr[...] = s0_ref[0]

    ri = lax.broadcasted_iota(jnp.int32, (chunk, chunk), 0)
    ci = lax.broadcasted_iota(jnp.int32, (chunk, chunk), 1)
    tril = ri >= ci
    stril = ri > ci
    eye = (ri == ci).astype(F32)
    nsq = int(round(math.log2(chunk))) - 1

    def mm1(a, b, dims=NN_DIMS):
        return lax.dot_general(a.astype(BF16), b.astype(BF16), dims, preferred_element_type=F32)

    hs = range(DN_HEADS)
    cols = [slice(h * DN_HEAD_DIM, (h + 1) * DN_HEAD_DIM) for h in hs]
    items = [(cc, h) for cc in range(cps) for h in hs]
    rows = [slice(cc * chunk, (cc + 1) * chunk) for cc in range(cps)]
    gb = [gb_ref[0, rows[cc], :] for cc in range(cps)]
    gcum = [jnp.dot(tril.astype(F32), gb[cc], precision=HIGHEST, preferred_element_type=F32)
            for cc in range(cps)]
    gcum_rows = [lax.dot_general(gcum[cc], eye, TN_DIMS, precision=HIGHEST, preferred_element_type=F32)
                 for cc in range(cps)]
    q = {it: q_ref[0, rows[it[0]], cols[it[1]]] for it in items}
    k = {it: k_ref[0, rows[it[0]], cols[it[1]]] for it in items}
    v = {it: v_ref[0, rows[it[0]], cols[it[1]]] for it in items}
    beta = {(cc, h): gb[cc][:, h:h + 1] for cc, h in items}
    gc = {(cc, h): gcum[cc][:, DN_HEADS + h:DN_HEADS + h + 1] for cc, h in items}
    kbeta = {it: k[it] * beta[it] for it in items}
    kk = {it: _mm3(kbeta[it], k[it], NT_DIMS) for it in items}
    qk = {it: mm1(q[it], k[it], NT_DIMS) for it in items}
    decay = {}
    for cc, h in items:
        diff = gc[cc, h] - gcum_rows[cc][DN_HEADS + h:DN_HEADS + h + 1, :]
        decay[cc, h] = jnp.where(tril, jnp.exp(jnp.where(tril, diff, 0.0)), 0.0)
    eg = {it: jnp.exp(gc[it]) for it in items}
    pw = {it: -jnp.where(stril, kk[it] * decay[it], 0.0) for it in items}
    y = {it: jnp.concatenate([v[it] * beta[it], kbeta[it] * eg[it]], axis=-1) for it in items}
    y = {it: y[it] + _mm3(pw[it], y[it]) for it in items}
    for _ in range(nsq):
        pw = {it: _mm3(pw[it], pw[it]) for it in items}
        y = {it: y[it] + _mm3(pw[it], y[it]) for it in items}

    state = [s_scr[h] for h in hs]
    for cc in range(cps):
        its = [(cc, h) for h in hs]
        v_new = [y[it][:, 0:DN_HEAD_DIM] - _mm3(y[it][:, DN_HEAD_DIM:2 * DN_HEAD_DIM], state[it[1]])
                 for it in its]
        g_last = [gc[it][chunk - 1:chunk, :] for it in its]
        kv = [_mm3(k[it] * jnp.exp(g_last[h] - gc[it]), v_new[h], TN_DIMS) for h, it in enumerate(its)]
        for h, it in enumerate(its):
            attn = jnp.where(tril, qk[it] * decay[it], 0.0)
            o_ref[0, rows[cc], cols[h]] = mm1(q[it] * eg[it], state[h]) + mm1(attn, v_new[h])
        state = [state[h] * jnp.exp(g_last[h]) + kv[h] for h in hs]
    for h in hs:
        s_scr[h] = state[h]

    @pl.when(c == nc - 1)
    def _():
        sout_ref[0] = s_scr[...]


def _gdn_call(qd, kd, vd, gb, s0, s0_index, chunk, cps):
    n, t, _ = qd.shape
    step_rows = chunk * cps
    nc = t // step_rows
    row = lambda n_, c_: (n_, c_, 0)
    blk = pl.BlockSpec((1, step_rows, DN_WIDTH), row)
    sshape = (DN_HEADS, DN_HEAD_DIM, DN_HEAD_DIM)
    s0_block = (None,) * (s0.ndim - 4) + (1,) + sshape
    return pl.pallas_call(
        functools.partial(_gdn_kernel, chunk=chunk, cps=cps),
        grid=(n, nc),
        in_specs=[blk, blk, blk,
                  pl.BlockSpec((1, step_rows, LANES), row),
                  pl.BlockSpec(s0_block, lambda n_, c_: s0_index(n_))],
        out_specs=[blk, pl.BlockSpec((1,) + sshape, lambda n_, c_: (n_, 0, 0, 0))],
        out_shape=[jax.ShapeDtypeStruct((n, t, DN_WIDTH), F32),
                   jax.ShapeDtypeStruct((n,) + sshape, F32)],
        scratch_shapes=[pltpu.VMEM(sshape, F32)],
        compiler_params=_cparams(("arbitrary", "arbitrary")),
        name="gdn",
    )(qd, kd, vd, gb, s0)


def _out_kernel(oa_ref, sza_ref, od_ref, szd_ref, x_ref, gate_ref, wo_ref, nw_ref, lg_ref, lb_ref,
                o_ref, *, alpha):
    ya = (oa_ref[0] * sza_ref[0]).astype(BF16)
    od = od_ref[0]
    parts = []
    for h in range(DN_HEADS):
        oh = od[:, h * DN_HEAD_DIM:(h + 1) * DN_HEAD_DIM]
        ms = jnp.mean(oh * oh, axis=-1, keepdims=True)
        parts.append(oh * lax.rsqrt(ms + RMS_EPS) * nw_ref[...])
    yd = (jnp.concatenate(parts, axis=-1) * szd_ref[0]).astype(BF16)
    y = (jnp.dot(ya, wo_ref[0:ATT_WIDTH, :], preferred_element_type=F32)
         + jnp.dot(yd, wo_ref[ATT_WIDTH:ATT_WIDTH + DN_WIDTH, :], preferred_element_type=F32))
    r = alpha * x_ref[0] + (1.0 + gate_ref[0]) * y
    mu = jnp.mean(r, axis=-1, keepdims=True)
    rc = r - mu
    var = jnp.mean(rc * rc, axis=-1, keepdims=True)
    o_ref[0] = rc * lax.rsqrt(var + LN_EPS) * lg_ref[...] + lb_ref[...]


def _out_call(oa, sza, od, szd, x, gate, w_out_bf, nw, lg, lb, tm, alpha):
    n, t, d = x.shape
    nt = t // tm
    row = lambda n_, t_: (n_, t_, 0)
    full = lambda n_, t_: (0, 0)
    blk = pl.BlockSpec((1, tm, ATT_WIDTH), row)
    if gate.shape[1] == 1:
        gate_spec = pl.BlockSpec((1, 1, d), lambda n_, t_: (n_, 0, 0))
    else:
        gate_spec = pl.BlockSpec((1, tm, d), row)
    return pl.pallas_call(
        functools.partial(_out_kernel, alpha=alpha),
        grid=(n, nt),
        in_specs=[blk, blk, blk, blk,
                  pl.BlockSpec((1, tm, d), row),
                  gate_spec,
                  pl.BlockSpec((ATT_WIDTH + DN_WIDTH, d), full),
                  pl.BlockSpec((1, DN_HEAD_DIM), full),
                  pl.BlockSpec((1, d), full),
                  pl.BlockSpec((1, d), full)],
        out_specs=pl.BlockSpec((1, tm, d), row),
        out_shape=jax.ShapeDtypeStruct((n, t, d), F32),
        compiler_params=_cparams(("arbitrary", "arbitrary")),
        name="out_proj",
    )(oa, sza, od, szd, x, gate, w_out_bf, nw, lg, lb)


def _rope_tables(pos):
    half = ATT_HEAD_DIM // 2
    inv_freq = ROPE_THETA ** (-jnp.arange(half, dtype=F32) / half)
    ang = pos.astype(F32)[:, None] * inv_freq[None, :]
    cos = jnp.cos(ang)
    sin = jnp.sin(ang)
    reps = LANES // ATT_HEAD_DIM
    return jnp.tile(cos, (1, 2 * reps)), jnp.tile(jnp.concatenate([-sin, sin], axis=1), (1, reps))


def _lane_row(vec, offset):
    return jnp.zeros((1, LANES), F32).at[0, offset:offset + vec.shape[0]].set(vec.astype(F32))


def kernel(x_prompt, x_sample, c_prompt, c_sample, cache_k, cache_v, page_table, state_ssm, state_conv,
           w_ada, b_ada, w_in, conv_w, a_log, dt_bias, dn_norm_w, w_out, ln_g, ln_b):
    depth, d, _ = w_ada.shape
    bp, seq, _ = x_prompt.shape
    bs, dec, _ = x_sample.shape
    npages = page_table.shape[1]
    past = npages * PAGE_SIZE
    n_pool = cache_k.shape[1]
    alpha = float((2 * depth) ** 0.25)
    tm = MOBA_BLOCK
    nb = seq // tm
    assert seq % tm == 0 and past % MOBA_BLOCK == 0 and nb <= MAX_MOBA_BLOCKS and nb % TILE_BLOCKS == 0
    assert seq % (DN_CHUNK * GDN_CHUNKS_PER_STEP) == 0
    assert dec <= SUBLANES and npages % PAGES_PER_STEP == 0

    nc_rows = bp + bs
    c_rows = -(-nc_rows // SUBLANES) * SUBLANES
    c_all = jnp.zeros((c_rows, d), F32).at[:bp].set(c_prompt).at[bp:nc_rows].set(c_sample)
    mod_all = _mod_call(c_all, w_ada, b_ada)

    cos_p, sin_p = _rope_tables(jnp.arange(seq, dtype=jnp.int32))
    pos_s = past + jnp.repeat(jnp.arange(dec, dtype=jnp.int32), bs)
    cos_s, sin_s = _rope_tables(pos_s)

    cache_kt = cache_k.transpose(0, 1, 3, 4, 2)
    cache_vt = cache_v.transpose(0, 1, 3, 4, 2)
    page_flat = page_table.reshape(-1).astype(jnp.int32)
    zero_state = jnp.zeros((bp, DN_HEADS, DN_HEAD_DIM, DN_HEAD_DIM), F32)

    def to_bm8(a):
        w = a.shape[-1]
        a = a.reshape(dec, bs, w).transpose(1, 0, 2)
        return jnp.pad(a, ((0, 0), (0, SUBLANES - dec), (0, 0)))

    def to_heads(a):
        a = a.reshape(dec, bs, ATT_HEADS, ATT_HEAD_DIM).transpose(1, 2, 0, 3)
        return jnp.pad(a, ((0, 0), (0, 0), (0, TOK_ROWS - dec), (0, 0)))

    xp = x_prompt
    xs_tm = x_sample.transpose(1, 0, 2).reshape(dec * bs, d)
    outs = [[] for _ in range(8)]
    for l in range(depth):
        w_main = w_in[l, :, :MAIN_COLS].astype(BF16)
        w_small = jnp.zeros((d, LANES), F32).at[:, :2 * DN_HEADS].set(w_in[l, :, MAIN_COLS:])
        alog = _lane_row(a_log[l], DN_HEADS)
        dtb = _lane_row(dt_bias[l], DN_HEADS)
        w_out_bf = w_out[l].astype(BF16)
        nw = dn_norm_w[l].reshape(1, DN_HEAD_DIM)
        lg = ln_g[l].reshape(1, d)
        lb = ln_b[l].reshape(1, d)
        mod_p = mod_all[l, :bp].reshape(bp, 1, 3 * d)
        mod_s = mod_all[l, bp:nc_rows]

        (q_aug, k, v, k_aug, vt, kmean, sza, qd, kd, vd, szd, gb, nconv) = _in_prompt_call(
            xp, mod_p, w_main, w_small, conv_w[l], alog, dtb, cos_p, sin_p, tm)
        km_heads = kmean.reshape(bp, nb, ATT_HEADS, ATT_HEAD_DIM).transpose(0, 2, 1, 3)
        km_pad = jnp.pad(km_heads, ((0, 0), (0, 0), (ATT_HEAD_DIM, LANES - ATT_HEAD_DIM - nb),
                                    (0, LANES - ATT_HEAD_DIM)))
        oa = _moba_prompt_call(q_aug, k_aug, vt, km_pad)
        od, s_new = _gdn_call(qd, kd, vd, gb, zero_state, lambda n_: (n_, 0, 0, 0), DN_CHUNK,
                              GDN_CHUNKS_PER_STEP)
        xp = _out_call(oa, sza, od, szd, xp, mod_p[:, :, 2 * d:], w_out_bf, nw, lg, lb, tm, alpha)
        outs[0].append(k.reshape(bp, seq, ATT_HEADS, ATT_HEAD_DIM))
        outs[1].append(v.reshape(bp, seq, ATT_HEADS, ATT_HEAD_DIM))
        outs[2].append(s_new)
        outs[3].append(nconv[:, SUBLANES - (CONV_WIDTH - 1):, :])

        mod_rows = jnp.tile(mod_s, (dec, 1))
        hist_tm = state_conv[l].transpose(1, 0, 2).reshape((CONV_WIDTH - 1) * bs, DN_QKV)
        (q, k, v, sza, qd, kd, vd, szd, gb, nconv) = _in_sample_call(
            xs_tm, mod_rows[:, :d], mod_rows[:, d:2 * d], w_main, w_small, conv_w[l], alog, dtb,
            cos_s, sin_s, hist_tm, bs, dec)
        k8, v8 = to_bm8(k), to_bm8(v)
        oh = _moba_sample_call(page_flat, to_heads(q), to_heads(k), to_heads(v), cache_kt, cache_vt,
                               l, npages, dec)
        oa8 = oh.transpose(0, 2, 1, 3).reshape(bs, SUBLANES, ATT_WIDTH)
        od8, s_new = _gdn_call(to_bm8(qd), to_bm8(kd), to_bm8(vd), to_bm8(gb), state_ssm,
                               lambda n_, l=l: (l, n_, 0, 0, 0), SUBLANES, 1)
        rows8 = bs * SUBLANES
        flat = lambda a: a.reshape(1, rows8, a.shape[-1])
        gate_rows = jnp.repeat(mod_s[:, 2 * d:], SUBLANES, axis=0).reshape(1, rows8, d)
        xs8 = _out_call(flat(oa8), flat(to_bm8(sza)), flat(od8), flat(to_bm8(szd)), flat(to_bm8(xs_tm)),
                        gate_rows, w_out_bf, nw, lg, lb, rows8, alpha)
        xs_bm = xs8.reshape(bs, SUBLANES, d)[:, :dec]
        xs_tm = xs_bm.transpose(1, 0, 2).reshape(dec * bs, d)
        outs[4].append(k8[:, :dec].reshape(bs, dec, ATT_HEADS, ATT_HEAD_DIM))
        outs[5].append(v8[:, :dec].reshape(bs, dec, ATT_HEADS, ATT_HEAD_DIM))
        outs[6].append(s_new)
        outs[7].append(nconv.reshape(CONV_WIDTH - 1, bs, DN_QKV).transpose(1, 0, 2))

    stacked = [jnp.stack(o) for o in outs]
    return (xp, xs_bm, stacked[0], stacked[1], stacked[2], stacked[3],
            stacked[4], stacked[5], stacked[6], stacked[7])
```
